```python
import jax, jax.numpy as jnp
from jax import lax
import numpy as np

D_MODEL = 1024
BATCH = 8
SEQ = 2048
DEPTH = 2
DEC_BATCH = 32
DEC_SEQ = 4
PAST_LEN = 8192
PAGE_SIZE = 128

EXPAND = 2
D_MIX = EXPAND * D_MODEL
W_CONV = D_MIX // 4
W_POOL = D_MIX // 4
W_SB = D_MIX // 4
W_MEM = D_MIX - W_CONV - W_POOL - W_SB
CONV_K = 31
POOL_WINDOWS = (2, 4, 8, 16)
N_POOL_GROUPS = len(POOL_WINDOWS)
POOL_GROUP = W_POOL // N_POOL_GROUPS
POOL_HIST = max(POOL_WINDOWS) - 1
SB_HEADS = 8
SB_HEAD_DIM = W_SB // SB_HEADS
SB_BIAS_INIT = -8.0
MEM_HEADS = 4
MEM_HEAD_DIM = W_MEM // MEM_HEADS
N_MEM = 256
Q_BLOCK = 128
EPS = 1e-6
SPLIT_SIZES = (W_CONV, W_CONV, W_CONV, W_POOL, W_POOL, W_SB, W_SB, W_SB, W_SB, W_MEM, W_MEM)
D_IN = sum(SPLIT_SIZES)

kernel_name = "hybrid_conv_pool_stickbreak_memory_decoder_step"


def rms_norm(x, g):
    xf = x.astype(jnp.float32)
    y = xf * lax.rsqrt(jnp.mean(xf * xf, axis=-1, keepdims=True) + EPS)
    return (y * g.astype(jnp.float32)).astype(x.dtype)


def layer_norm(x, g, b):
    xf = x.astype(jnp.float32)
    mu = jnp.mean(xf, axis=-1, keepdims=True)
    var = jnp.mean(jnp.square(xf - mu), axis=-1, keepdims=True)
    y = (xf - mu) * lax.rsqrt(var + EPS) * g.astype(jnp.float32) + b.astype(jnp.float32)
    return y.astype(x.dtype)


def conv_branch(a, b, hist, w_dw, b_dw, ln_g, ln_b, w_pw, b_pw):
    u = a * jax.nn.sigmoid(b)
    u_ext = jnp.concatenate([hist.astype(u.dtype), u], axis=1)
    y = lax.conv_general_dilated(u_ext, w_dw[:, None, :].astype(u.dtype), (1,), 'VALID',
                                 dimension_numbers=('NWC', 'WIO', 'NWC'),
                                 feature_group_count=W_CONV) + b_dw
    y = jax.nn.silu(layer_norm(y, ln_g, ln_b))
    y = y @ w_pw + b_pw
    return y, u_ext[:, -(CONV_K - 1):]


def pool_branch(u, hist, pos, w_pool, scale):
    n, l, _ = u.shape
    u_ext = jnp.concatenate([hist.astype(u.dtype), u], axis=1)
    c = jnp.cumsum(u_ext.astype(jnp.float32), axis=1)
    c = jnp.pad(c, ((0, 0), (1, 0), (0, 0)))
    end = c[:, POOL_HIST + 1:]
    means = []
    for gi, w in enumerate(POOL_WINDOWS):
        sl = slice(gi * POOL_GROUP, (gi + 1) * POOL_GROUP)
        start = c[:, POOL_HIST + 1 - w: POOL_HIST + 1 - w + l, sl]
        cnt = jnp.minimum(w, pos + 1).astype(jnp.float32)[None, :, None]
        means.append((end[..., sl] - start) / cnt)
    mean = jnp.concatenate(means, axis=-1).astype(u.dtype)
    d = (mean - u).reshape(n, l, N_POOL_GROUPS, POOL_GROUP)
    y = jnp.einsum('nlgc,gcd->nlgd', d, w_pool).reshape(n, l, W_POOL) * scale
    return y, u_ext[:, -POOL_HIST:]


def sb_block(q, k, v, bias, q_pos, k_pos):
    z = (jnp.einsum('nqhd,nkhd->nhqk', q, k).astype(jnp.float32) * (SB_HEAD_DIM ** -0.5)
         + bias.astype(jnp.float32)[None, :, None, None])
    mask = (k_pos[None, :] < q_pos[:, None])[None, None]
    log_beta = jax.nn.log_sigmoid(z)
    log_1m = jnp.where(mask, jax.nn.log_sigmoid(-z), 0.0)
    rc = lax.cumsum(log_1m, axis=3, reverse=True)
    a = jnp.exp(jnp.where(mask, log_beta + rc - log_1m, -jnp.inf))
    return jnp.einsum('nhqk,nkhd->nqhd', a.astype(v.dtype), v)


def sb_prompt(q, k, v, bias):
    n, l, h, dh = q.shape
    nb = l // Q_BLOCK
    qb = q.reshape(n, nb, Q_BLOCK, h, dh).transpose(1, 0, 2, 3, 4)
    pos = jnp.arange(l)
    qpos = pos.reshape(nb, Q_BLOCK)
    out = lax.map(lambda args: sb_block(args[0], k, v, bias, args[1], pos), (qb, qpos))
    return out.transpose(1, 0, 2, 3, 4).reshape(n, l, h, dh)


def mem_attend(q, mk, mv):
    s = jnp.einsum('nqhd,nmhd->nhqm', q, mk).astype(jnp.float32) * (MEM_HEAD_DIM ** -0.5)
    p = jax.nn.softmax(s, axis=-1)
    return jnp.einsum('nhqm,nmhd->nqhd', p.astype(mv.dtype), mv)


def mem_kv(mem, g, w):
    n = mem.shape[0]
    kv = rms_norm(mem, g) @ w
    mk, mv = jnp.split(kv, 2, axis=-1)
    return (mk.reshape(n, -1, MEM_HEADS, MEM_HEAD_DIM), mv.reshape(n, -1, MEM_HEADS, MEM_HEAD_DIM))


def mixer_layer(x, pos, conv_hist, pool_hist, sb_fn, mk, mv, pre_g, post_g, w_in,
                w_dw, b_dw, ln_g, ln_b, w_pw, b_pw, pool_w, pool_scale, sb_bias, w_out):
    n, l, _ = x.shape
    h = rms_norm(x, pre_g)
    z = h @ w_in
    ca, cb, cg, pu, pg, q, k, v, sg, mq, mg = jnp.split(
        z, np.cumsum(SPLIT_SIZES)[:-1].tolist(), axis=-1)
    y_c, conv_new = conv_branch(ca, cb, conv_hist, w_dw, b_dw, ln_g, ln_b, w_pw, b_pw)
    y_p, pool_new = pool_branch(pu, pool_hist, pos, pool_w, pool_scale)
    q = q.reshape(n, l, SB_HEADS, SB_HEAD_DIM)
    k = k.reshape(n, l, SB_HEADS, SB_HEAD_DIM)
    v = v.reshape(n, l, SB_HEADS, SB_HEAD_DIM)
    y_s = sb_fn(q, k, v, sb_bias).reshape(n, l, W_SB)
    y_m = mem_attend(mq.reshape(n, l, MEM_HEADS, MEM_HEAD_DIM), mk, mv).reshape(n, l, W_MEM)
    mix = jnp.concatenate([y_c * jax.nn.silu(cg), y_p * jax.nn.silu(pg),
                           y_s * jax.nn.silu(sg), y_m * jax.nn.silu(mg)], axis=-1)
    out = rms_norm(mix @ w_out, post_g)
    return x + out, conv_new, pool_new, k, v


def setup_inputs(seed: int = 0) -> dict:
    key = jax.random.key(seed)
    ks = jax.random.split(key, 32)
    f32 = jnp.float32
    n_pages = PAST_LEN // PAGE_SIZE
    n_used = DEC_BATCH * n_pages
    n_phys = (5 * n_used) // 4

    def nrm(k, shape, s=1.0):
        return s * jax.random.normal(k, shape, f32)

    page_table = jax.random.permutation(ks[0], n_phys)[:n_used].reshape(DEC_BATCH, n_pages).astype(jnp.int32)
    return {
        "x_prompt": nrm(ks[1], (BATCH, SEQ, D_MODEL)),
        "x_sample": nrm(ks[2], (DEC_BATCH, DEC_SEQ, D_MODEL)),
        "mem_prompt": nrm(ks[3], (BATCH, N_MEM, D_MODEL)),
        "cache_k": nrm(ks[4], (DEPTH, n_phys, PAGE_SIZE, SB_HEADS, SB_HEAD_DIM)),
        "cache_v": nrm(ks[5], (DEPTH, n_phys, PAGE_SIZE, SB_HEADS, SB_HEAD_DIM)),
        "cache_mem_k": nrm(ks[6], (DEPTH, DEC_BATCH, N_MEM, MEM_HEADS, MEM_HEAD_DIM)),
        "cache_mem_v": nrm(ks[7], (DEPTH, DEC_BATCH, N_MEM, MEM_HEADS, MEM_HEAD_DIM)),
        "state_conv": nrm(ks[8], (DEPTH, DEC_BATCH, CONV_K - 1, W_CONV), 0.5),
        "state_pool": nrm(ks[9], (DEPTH, DEC_BATCH, POOL_HIST, W_POOL)),
        "page_table": page_table,
        "pre_g": 1.0 + nrm(ks[10], (DEPTH, D_MODEL), 0.05),
        "post_g": 1.0 + nrm(ks[11], (DEPTH, D_MODEL), 0.05),
        "w_in": nrm(ks[12], (DEPTH, D_MODEL, D_IN), D_MODEL ** -0.5),
        "conv_w_dw": nrm(ks[13], (DEPTH, CONV_K, W_CONV), CONV_K ** -0.5),
        "conv_b_dw": nrm(ks[14], (DEPTH, W_CONV), 0.02),
        "conv_ln_g": 1.0 + nrm(ks[15], (DEPTH, W_CONV), 0.05),
        "conv_ln_b": nrm(ks[16], (DEPTH, W_CONV), 0.02),
        "conv_w_pw": nrm(ks[17], (DEPTH, W_CONV, W_CONV), W_CONV ** -0.5),
        "conv_b_pw": nrm(ks[18], (DEPTH, W_CONV), 0.02),
        "pool_w": nrm(ks[19], (DEPTH, N_POOL_GROUPS, POOL_GROUP, POOL_GROUP), POOL_GROUP ** -0.5),
        "pool_scale": 1.0 + nrm(ks[20], (DEPTH, W_POOL), 0.05),
        "sb_bias": SB_BIAS_INIT + nrm(ks[25], (DEPTH, SB_HEADS), 0.1),
        "mem_g": 1.0 + nrm(ks[21], (DEPTH, D_MODEL), 0.05),
        "w_mem_kv": nrm(ks[22], (DEPTH, D_MODEL, 2 * W_MEM), D_MODEL ** -0.5),
        "w_out": nrm(ks[23], (DEPTH, D_MIX, D_MODEL), D_MIX ** -0.5),
        "final_g": 1.0 + nrm(ks[24], (D_MODEL,), 0.05),
    }


def reference(x_prompt, x_sample, mem_prompt, cache_k, cache_v, cache_mem_k, cache_mem_v,
              state_conv, state_pool, page_table, pre_g, post_g, w_in, conv_w_dw, conv_b_dw,
              conv_ln_g, conv_ln_b, conv_w_pw, conv_b_pw, pool_w, pool_scale, sb_bias, mem_g,
              w_mem_kv, w_out, final_g):
    bp, lp, _ = x_prompt.shape
    bs, ls, _ = x_sample.shape
    past = page_table.shape[1] * PAGE_SIZE
    pos_p = jnp.arange(lp)
    pos_s = past + jnp.arange(ls)
    k_pos_s = jnp.arange(past + ls)

    xp, xs = x_prompt, x_sample
    kp_l, vp_l, ks_l, vs_l = [], [], [], []
    cp_l, cs_l, pp_l, ps_l, mk_l, mv_l = [], [], [], [], [], []
    for l in range(DEPTH):
        lw = (pre_g[l], post_g[l], w_in[l], conv_w_dw[l], conv_b_dw[l], conv_ln_g[l], conv_ln_b[l],
              conv_w_pw[l], conv_b_pw[l], pool_w[l], pool_scale[l], sb_bias[l], w_out[l])
        mk_p, mv_p = mem_kv(mem_prompt, mem_g[l], w_mem_kv[l])
        xp, c_new, p_new, k_new, v_new = mixer_layer(
            xp, pos_p, jnp.zeros((bp, CONV_K - 1, W_CONV), xp.dtype),
            jnp.zeros((bp, POOL_HIST, W_POOL), xp.dtype), sb_prompt, mk_p, mv_p, *lw)
        kp_l.append(k_new); vp_l.append(v_new); cp_l.append(c_new); pp_l.append(p_new)
        mk_l.append(mk_p); mv_l.append(mv_p)
        k_past = cache_k[l][page_table].reshape(bs, past, SB_HEADS, SB_HEAD_DIM)
        v_past = cache_v[l][page_table].reshape(bs, past, SB_HEADS, SB_HEAD_DIM)

        def sb_sample(q, k, v, bias, k_past=k_past, v_past=v_past):
            return sb_block(q, jnp.concatenate([k_past, k], axis=1),
                            jnp.concatenate([v_past, v], axis=1), bias, pos_s, k_pos_s)

        xs, c_new, p_new, k_new, v_new = mixer_layer(
            xs, pos_s, state_conv[l], state_pool[l], sb_sample, cache_mem_k[l], cache_mem_v[l], *lw)
        ks_l.append(k_new); vs_l.append(v_new); cs_l.append(c_new); ps_l.append(p_new)

    y_prompt = rms_norm(xp, final_g)
    y_sample = rms_norm(xs, final_g)
    return (y_prompt, y_sample,
            jnp.stack(kp_l), jnp.stack(vp_l), jnp.stack(ks_l), jnp.stack(vs_l),
            jnp.stack(cp_l), jnp.stack(cs_l), jnp.stack(pp_l), jnp.stack(ps_l),
            jnp.stack(mk_l), jnp.stack(mv_l))
```

```python
import functools

import jax
import jax.numpy as jnp
from jax import lax
from jax.experimental import pallas as pl
from jax.experimental.pallas import tpu as pltpu

F32 = jnp.float32
BF16 = jnp.bfloat16

EPS = 1e-6
GROUP_W = 512
MIX_GROUPS = ("conv_a", "conv_glu_gate", "conv_silu_gate", "pool_u", "pool_gate", "sb_q", "sb_k",
              "sb_v", "sb_gate", "mem_q", "mem_gate")
K_GROUP = MIX_GROUPS.index("sb_k")
V_GROUP = MIX_GROUPS.index("sb_v")
CONV_K = 31
CONV_HIST = CONV_K - 1
POOL_WINDOWS = (2, 4, 8, 16)
POOL_GROUP = 128
POOL_HIST = max(POOL_WINDOWS) - 1
SB_HEADS = 8
SB_HEAD_DIM = 64
MEM_HEADS = 4
MEM_HEAD_DIM = 128
PAGE_SIZE = 128
LANES = 128
SB_BLOCK = 128
VMEM_LIMIT_BYTES = 56 * 1024 * 1024


def _params(*semantics):
    return pltpu.CompilerParams(dimension_semantics=semantics, vmem_limit_bytes=VMEM_LIMIT_BYTES)


def _sigmoid(x):
    return 1.0 / (1.0 + jnp.exp(-x))


def _silu(x):
    return x * _sigmoid(x)


def _rms(x, g):
    return x * lax.rsqrt(jnp.mean(x * x, axis=-1, keepdims=True) + EPS) * g


def _rms_proj_body(x_ref, g_ref, w_ref, *o_refs, plan):
    h = _rms(x_ref[...], g_ref[...]).astype(BF16)
    done = {}
    for (c, kind), o_ref in zip(plan, o_refs):
        if c not in done:
            done = {c: jnp.dot(h, w_ref[:, c * GROUP_W:(c + 1) * GROUP_W],
                               preferred_element_type=F32)}
        y = done[c]
        if kind == "rows":
            o_ref[...] = y.astype(o_ref.dtype)
        else:
            o_ref[0] = y.T


def rms_proj(x, g, w, tm, plan=None, seq_len=None):
    m, d = x.shape
    n = w.shape[1]
    if plan is None:
        plan = tuple((c, "rows", F32) for c in range(n // GROUP_W))
    out_specs, out_shape = [], []
    for _, kind, dtype in plan:
        if kind == "rows":
            out_specs.append(pl.BlockSpec((tm, GROUP_W), lambda i: (i, 0)))
            out_shape.append(jax.ShapeDtypeStruct((m, GROUP_W), dtype))
        else:
            per_seq = seq_len // tm
            out_specs.append(pl.BlockSpec((1, GROUP_W, tm), lambda i: (i // per_seq, 0, i % per_seq)))
            out_shape.append(jax.ShapeDtypeStruct((m // seq_len, GROUP_W, seq_len), dtype))
    return pl.pallas_call(
        functools.partial(_rms_proj_body, plan=tuple((c, kind) for c, kind, _ in plan)),
        grid=(m // tm,),
        in_specs=[pl.BlockSpec((tm, d), lambda i: (i, 0)),
                  pl.BlockSpec((1, d), lambda i: (0, 0)),
                  pl.BlockSpec((d, n), lambda i: (0, 0))],
        out_specs=out_specs,
        out_shape=out_shape,
        compiler_params=_params("parallel"),
        name="rms_proj",
    )(x, g.reshape(1, d), w)


def _conv_body(a_ref, b_ref, g_ref, hist_ref, wdw_ref, bdw_ref, lng_ref, lnb_ref, wpw_ref, bpw_ref,
               y_ref, new_ref, ext_ref, s_ref, *, t, rows):
    ti = pl.program_id(1)
    lead = 32 - CONV_HIST

    @pl.when(ti == 0)
    def _():
        ext_ref[lead:32, :] = hist_ref[0]

    @pl.when(ti > 0)
    def _():
        ext_ref[lead:32, :] = ext_ref[t + lead:t + 32, :]

    ext_ref[32:32 + t, :] = a_ref[0] * _sigmoid(b_ref[0])

    for c in range(t // rows):
        base = c * rows
        acc = jnp.broadcast_to(bdw_ref[...], (rows, GROUP_W))
        for j in range(CONV_K):
            acc = acc + wdw_ref[j:j + 1, :] * ext_ref[base + lead + j:base + lead + j + rows, :]
        mu = jnp.mean(acc, axis=-1, keepdims=True)
        xc = acc - mu
        var = jnp.mean(xc * xc, axis=-1, keepdims=True)
        yn = xc * lax.rsqrt(var + EPS) * lng_ref[...] + lnb_ref[...]
        s_ref[base:base + rows, :] = _silu(yn).astype(BF16)

    y = jnp.dot(s_ref[...], wpw_ref[...], preferred_element_type=F32) + bpw_ref[...]
    y_ref[0] = (y * _silu(g_ref[0])).astype(y_ref.dtype)

    @pl.when(ti == pl.num_programs(1) - 1)
    def _():
        new_ref[0] = ext_ref[t + lead:t + 32, :]


def conv_mixer(a, b, g, hist, w_dw, b_dw, ln_g, ln_b, w_pw, b_pw, t, out_dtype):
    n, l, w = a.shape
    rows = min(t, 32)
    tile = pl.BlockSpec((1, t, w), lambda i, j: (i, j, 0))
    vec = pl.BlockSpec((1, w), lambda i, j: (0, 0))
    return pl.pallas_call(
        functools.partial(_conv_body, t=t, rows=rows),
        grid=(n, l // t),
        in_specs=[tile, tile, tile,
                  pl.BlockSpec((1, CONV_HIST, w), lambda i, j: (i, 0, 0)),
                  pl.BlockSpec((CONV_K, w), lambda i, j: (0, 0)),
                  vec, vec, vec,
                  pl.BlockSpec((w, w), lambda i, j: (0, 0)),
                  vec],
        out_specs=[tile, pl.BlockSpec((1, CONV_HIST, w), lambda i, j: (i, 0, 0))],
        out_shape=[jax.ShapeDtypeStruct((n, l, w), out_dtype),
                   jax.ShapeDtypeStruct((n, CONV_HIST, w), F32)],
        scratch_shapes=[pltpu.VMEM((32 + t, w), F32), pltpu.VMEM((t, w), BF16)],
        compiler_params=_params("parallel", "arbitrary"),
        name="conv_mixer",
    )(a, b, g, hist, w_dw, b_dw.reshape(1, w), ln_g.reshape(1, w), ln_b.reshape(1, w),
      w_pw, b_pw.reshape(1, w))


def _pool_body(u_ref, g_ref, hist_ref, w_ref, scale_ref, y_ref, new_ref, ext_ref, *, t, pos0):
    ti = pl.program_id(1)
    lead = 16 - POOL_HIST

    @pl.when(ti == 0)
    def _():
        ext_ref[lead:16, :] = hist_ref[0]

    @pl.when(ti > 0)
    def _():
        ext_ref[lead:16, :] = ext_ref[t + lead:t + 16, :]

    ext_ref[16:16 + t, :] = u_ref[0]
    pos = pos0 + ti * t + lax.broadcasted_iota(jnp.int32, (t, POOL_GROUP), 0)
    for gi, win in enumerate(POOL_WINDOWS):
        sl = slice(gi * POOL_GROUP, (gi + 1) * POOL_GROUP)
        u = u_ref[0, :, sl]
        tot = u
        for k in range(1, win):
            tot = tot + ext_ref[16 - k:16 - k + t, sl]
        cnt = jnp.minimum(win, pos + 1).astype(F32)
        d = tot / cnt - u
        y = jnp.dot(d.astype(BF16), w_ref[gi], preferred_element_type=F32) * scale_ref[:, sl]
        y_ref[0, :, sl] = (y * _silu(g_ref[0, :, sl])).astype(y_ref.dtype)

    @pl.when(ti == pl.num_programs(1) - 1)
    def _():
        new_ref[0] = ext_ref[t + lead:t + 16, :]


def pool_mixer(u, g, hist, w_pool, scale, t, pos0, out_dtype):
    n, l, w = u.shape
    tile = pl.BlockSpec((1, t, w), lambda i, j: (i, j, 0))
    return pl.pallas_call(
        functools.partial(_pool_body, t=t, pos0=pos0),
        grid=(n, l // t),
        in_specs=[tile, tile,
                  pl.BlockSpec((1, POOL_HIST, w), lambda i, j: (i, 0, 0)),
                  pl.BlockSpec(w_pool.shape, lambda i, j: (0, 0, 0)),
                  pl.BlockSpec((1, w), lambda i, j: (0, 0))],
        out_specs=[tile, pl.BlockSpec((1, POOL_HIST, w), lambda i, j: (i, 0, 0))],
        out_shape=[jax.ShapeDtypeStruct((n, l, w), out_dtype),
                   jax.ShapeDtypeStruct((n, POOL_HIST, w), F32)],
        scratch_shapes=[pltpu.VMEM((16 + t, w), F32)],
        compiler_params=_params("parallel", "arbitrary"),
        name="pool_mixer",
    )(u, g, hist, w_pool, scale.reshape(1, w))


def _suffix_sum_matrix(blk):
    r = lax.broadcasted_iota(jnp.int32, (blk, blk), 0)
    c = lax.broadcasted_iota(jnp.int32, (blk, blk), 1)
    half = jnp.concatenate([(r >= c).astype(BF16), jnp.ones((blk, blk), BF16)], axis=1)
    return jnp.concatenate([half, half], axis=0)


def _sb_block(z, tri, carry, mask):
    blk = z.shape[1]
    lg = -(jnp.maximum(z, 0.0) + jnp.log(1.0 + jnp.exp(-jnp.abs(z))))
    if mask is not None:
        lg = jnp.where(mask, lg, 0.0)
    hi = lg.astype(BF16)
    lo = (lg - hi.astype(F32)).astype(BF16)
    r = jnp.dot(jnp.concatenate([hi, lo], axis=1), tri, preferred_element_type=F32)
    a = jnp.exp(z + r[:, :blk] + carry)
    if mask is not None:
        a = jnp.where(mask, a, 0.0)
    return a, carry + r[:, blk:]


def _sb_prompt_body(bias_ref, q_ref, kt_ref, v_ref, g_ref, tri_ref, o_ref, carry_ref, acc_ref):
    blk = SB_BLOCK
    i = pl.program_id(1)
    even = lax.broadcasted_iota(jnp.int32, (blk, LANES), 1) < SB_HEAD_DIM
    top = lax.broadcasted_iota(jnp.int32, (2 * blk, blk), 0) < blk
    r = lax.broadcasted_iota(jnp.int32, (2 * blk, blk), 0)
    c = lax.broadcasted_iota(jnp.int32, (2 * blk, blk), 1)
    diag_mask = c < jnp.where(top, r, r - blk)
    tri = tri_ref[...]
    zero = jnp.zeros((), BF16)

    for p in range(SB_HEADS // 2):
        sl = slice(p * LANES, (p + 1) * LANES)
        qp = q_ref[0, :, sl] * (SB_HEAD_DIM ** -0.5)
        qq = jnp.concatenate([jnp.where(even, qp, 0.0), jnp.where(even, 0.0, qp)], axis=0).astype(BF16)
        bias = jnp.where(top, bias_ref[2 * p], bias_ref[2 * p + 1])

        def block(j, mask):
            start = pl.multiple_of(j * blk, blk)
            kt = kt_ref[0, sl, pl.ds(start, blk)].astype(BF16)
            vb = v_ref[0, pl.ds(start, blk), sl]
            z = jnp.dot(qq, kt, preferred_element_type=F32) + bias
            a, carry = _sb_block(z, tri, carry_ref[...], mask)
            carry_ref[...] = carry
            ab = a.astype(BF16)
            aa = jnp.concatenate([ab[:blk], ab[blk:]], axis=1)
            vv = jnp.concatenate([jnp.where(even, vb, zero), jnp.where(even, zero, vb)], axis=0)
            acc_ref[...] += jnp.dot(aa, vv, preferred_element_type=F32)

        carry_ref[...] = jnp.zeros_like(carry_ref)
        acc_ref[...] = jnp.zeros_like(acc_ref)
        block(i, diag_mask)

        def step(it, _):
            block(i - 1 - it, None)
            return 0

        lax.fori_loop(0, i, step, 0)
        o_ref[0, :, sl] = (acc_ref[...] * _silu(g_ref[0, :, sl])).astype(o_ref.dtype)


def sb_prompt(q, kt, v, g, bias):
    n, l, w = q.shape
    blk = SB_BLOCK
    tile = pl.BlockSpec((1, blk, w), lambda b, i: (b, i, 0))
    return pl.pallas_call(
        _sb_prompt_body,
        grid=(n, l // blk),
        in_specs=[pl.BlockSpec(memory_space=pltpu.SMEM), tile,
                  pl.BlockSpec((1, w, l), lambda b, i: (b, 0, 0)),
                  pl.BlockSpec((1, l, w), lambda b, i: (b, 0, 0)), tile,
                  pl.BlockSpec((2 * blk, 2 * blk), lambda b, i: (0, 0))],
        out_specs=tile,
        out_shape=jax.ShapeDtypeStruct((n, l, w), BF16),
        scratch_shapes=[pltpu.VMEM((2 * blk, blk), F32), pltpu.VMEM((blk, LANES), F32)],
        compiler_params=_params("parallel", "arbitrary"),
        name="sb_prompt",
    )(bias, q, kt, v, g, _suffix_sum_matrix(blk))


def _sb_sample_body(pt_ref, q_ref, kn_ref, vn_ref, g_ref, bias_ref, tri_ref, *rest, ls, pages):
    kt_refs = rest[:pages]
    vt_refs = rest[pages:2 * pages]
    o_ref = rest[2 * pages]
    qall_ref, carry_ref, acc_ref, pad_ref = rest[2 * pages + 1:]
    blk = PAGE_SIZE
    rows = ls * SB_HEADS
    w = SB_HEADS * SB_HEAD_DIM
    s = pl.program_id(1)
    nt = (((1,), (1,)), ((), ()))
    head_mask = (lax.broadcasted_iota(jnp.int32, (SB_HEADS, w), 1) // SB_HEAD_DIM
                 == lax.broadcasted_iota(jnp.int32, (SB_HEADS, w), 0))
    tri = tri_ref[...]

    def block(z, weighted_values, mask):
        a, carry = _sb_block(z + bias_ref[...], tri, carry_ref[...], mask)
        carry_ref[...] = carry
        acc_ref[...] += weighted_values(a.astype(BF16))

    @pl.when(s == 0)
    def _():
        for t in range(ls):
            qt = q_ref[0, t:t + 1, :] * (SB_HEAD_DIM ** -0.5)
            qall_ref[t * SB_HEADS:(t + 1) * SB_HEADS, :] = jnp.where(head_mask, qt, 0.0).astype(BF16)
        carry_ref[...] = jnp.zeros_like(carry_ref)
        acc_ref[...] = jnp.zeros_like(acc_ref)
        key = lax.broadcasted_iota(jnp.int32, (rows, blk), 1)
        qry = lax.broadcasted_iota(jnp.int32, (rows, blk), 0) // SB_HEADS
        pad_ref[...] = jnp.zeros_like(pad_ref)
        pad_ref[0:ls, :] = kn_ref[0]
        z = lax.dot_general(qall_ref[...], pad_ref[...].astype(BF16), nt, preferred_element_type=F32)
        pad_ref[0:ls, :] = vn_ref[0]
        v_new = pad_ref[...].astype(BF16)
        block(z, lambda a: jnp.dot(a, v_new, preferred_element_type=F32), key < qry)

    for r in range(pages):
        z = jnp.dot(qall_ref[...], kt_refs[r][0, 0].astype(BF16), preferred_element_type=F32)
        vt = vt_refs[r][0, 0].astype(BF16)
        block(z, lambda a, vt=vt: lax.dot_general(a, vt, nt, preferred_element_type=F32), None)

    @pl.when(s == pl.num_programs(1) - 1)
    def _():
        for t in range(ls):
            o = jnp.where(head_mask, acc_ref[t * SB_HEADS:(t + 1) * SB_HEADS, :], 0.0)
            o = jnp.sum(o, axis=0, keepdims=True)
            o_ref[0, t:t + 1, :] = (o * _silu(g_ref[0, t:t + 1, :])).astype(o_ref.dtype)


def sb_sample(q, k_new, v_new, g, bias, cache_kt, cache_vt, page_table, layer, pages):
    n, ls, w = q.shape
    n_pages = page_table.shape[1]
    rows = ls * SB_HEADS
    blk = PAGE_SIZE
    small = pl.BlockSpec((1, ls, w), lambda b, s, pt: (b, 0, 0))

    def page_spec(r):
        def index(b, s, pt):
            return (layer, pt[b * n_pages + n_pages - 1 - (s * pages + r)], 0, 0)
        return pl.BlockSpec((1, 1, w, blk), index)

    bias_rows = jnp.broadcast_to(jnp.tile(bias, ls)[:, None], (rows, blk)).astype(F32)
    grid_spec = pltpu.PrefetchScalarGridSpec(
        num_scalar_prefetch=1,
        grid=(n, n_pages // pages),
        in_specs=[small, small, small, small,
                  pl.BlockSpec((rows, blk), lambda b, s, pt: (0, 0)),
                  pl.BlockSpec((2 * blk, 2 * blk), lambda b, s, pt: (0, 0))]
                 + [page_spec(r) for r in range(pages)] * 2,
        out_specs=small,
        scratch_shapes=[pltpu.VMEM((rows, w), BF16), pltpu.VMEM((rows, blk), F32),
                        pltpu.VMEM((rows, w), F32), pltpu.VMEM((blk, w), F32)],
    )
    return pl.pallas_call(
        functools.partial(_sb_sample_body, ls=ls, pages=pages),
        grid_spec=grid_spec,
        out_shape=jax.ShapeDtypeStruct((n, ls, w), F32),
        compiler_params=_params("parallel", "arbitrary"),
        name="sb_sample",
    )(page_table.reshape(-1), q, k_new, v_new, g, bias_rows, _suffix_sum_matrix(blk),
      *([cache_kt] * pages), *([cache_vt] * pages))


def _mem_body(q_ref, g_ref, mk_ref, mv_ref, o_ref):
    nt = (((1,), (1,)), ((), ()))
    for h in range(MEM_HEADS):
        sl = slice(h * MEM_HEAD_DIM, (h + 1) * MEM_HEAD_DIM)
        s = lax.dot_general(q_ref[0, :, sl].astype(BF16), mk_ref[0, :, sl].astype(BF16), nt,
                            preferred_element_type=F32) * (MEM_HEAD_DIM ** -0.5)
        e = jnp.exp(s - jnp.max(s, axis=-1, keepdims=True))
        p = e / jnp.sum(e, axis=-1, keepdims=True)
        o = jnp.dot(p.astype(BF16), mv_ref[0, :, sl].astype(BF16), preferred_element_type=F32)
        o_ref[0, :, sl] = (o * _silu(g_ref[0, :, sl])).astype(o_ref.dtype)


def mem_attend(q, g, mk, mv, t, out_dtype):
    n, l, w = q.shape
    n_mem = mk.shape[1]
    tile = pl.BlockSpec((1, t, w), lambda b, i: (b, i, 0))
    mem = pl.BlockSpec((1, n_mem, w), lambda b, i: (b, 0, 0))
    return pl.pallas_call(
        _mem_body,
        grid=(n, l // t),
        in_specs=[tile, tile, mem, mem],
        out_specs=tile,
        out_shape=jax.ShapeDtypeStruct((n, l, w), out_dtype),
        compiler_params=_params("parallel", "arbitrary"),
        name="mem_attend",
    )(q, g, mk, mv)


def _out_proj_body(c_ref, p_ref, s_ref, m_ref, w_ref, pg_ref, x_ref, fg_ref, o_ref, *, final):
    acc = None
    for gi, part in enumerate((c_ref, p_ref, s_ref, m_ref)):
        d = jnp.dot(part[...].astype(BF16), w_ref[gi * GROUP_W:(gi + 1) * GROUP_W, :],
                    preferred_element_type=F32)
        acc = d if acc is None else acc + d
    y = x_ref[...] + _rms(acc, pg_ref[...])
    if final:
        y = _rms(y, fg_ref[...])
    o_ref[...] = y


def out_proj(parts, w, post_g, x, final_g, tm, final):
    m, d = x.shape
    part = pl.BlockSpec((tm, GROUP_W), lambda i: (i, 0))
    vec = pl.BlockSpec((1, d), lambda i: (0, 0))
    row = pl.BlockSpec((tm, d), lambda i: (i, 0))
    return pl.pallas_call(
        functools.partial(_out_proj_body, final=final),
        grid=(m // tm,),
        in_specs=[part, part, part, part, pl.BlockSpec(w.shape, lambda i: (0, 0)), vec, row, vec],
        out_specs=row,
        out_shape=jax.ShapeDtypeStruct((m, d), F32),
        compiler_params=_params("parallel"),
        name="out_proj",
    )(*parts, w, post_g.reshape(1, d), x, final_g.reshape(1, d))


def _mixer_layer(x, n, l, conv_hist, pool_hist, pos0, sb_fn, mk, mv, lw, final_g, final, tiles,
                 keys_on_lanes):
    (pre_g, post_g, w_in, w_dw, b_dw, ln_g, ln_b, w_pw, b_pw, pool_w, pool_scale, sb_bias, w_out) = lw
    tm, t_local, t_mem, mix_dtype = tiles
    plan = [(c, "rows", F32) for c in range(len(MIX_GROUPS))]
    if keys_on_lanes:
        plan[K_GROUP] = (K_GROUP, "cols", F32)
        plan[V_GROUP:V_GROUP + 1] = [(V_GROUP, "cols", F32), (V_GROUP, "rows", BF16)]
    z = list(rms_proj(x, pre_g, w_in, tm, tuple(plan), l))
    v_rows = z.pop(V_GROUP + 1) if keys_on_lanes else z[V_GROUP]
    ca, cb, cg, pu, pg, q, k, v, sg, mq, mg = [
        a if keys_on_lanes and i in (K_GROUP, V_GROUP) else a.reshape(n, l, GROUP_W)
        for i, a in enumerate(z)]
    v_rows = v_rows.reshape(n, l, GROUP_W)
    y_c, conv_new = conv_mixer(ca, cb, cg, conv_hist, w_dw, b_dw, ln_g, ln_b, w_pw, b_pw, t_local,
                               mix_dtype)
    y_p, pool_new = pool_mixer(pu, pg, pool_hist, pool_w, pool_scale, t_local, pos0, mix_dtype)
    y_s = sb_fn(q, k, v_rows, sg, sb_bias)
    y_m = mem_attend(mq, mg, mk, mv, t_mem, mix_dtype)
    parts = [a.reshape(n * l, GROUP_W) for a in (y_c, y_p, y_s, y_m)]
    x = out_proj(parts, w_out, post_g, x, final_g, tm, final)
    return x, conv_new, pool_new, k, v


def kernel(x_prompt, x_sample, mem_prompt, cache_k, cache_v, cache_mem_k, cache_mem_v, state_conv,
           state_pool, page_table, pre_g, post_g, w_in, conv_w_dw, conv_b_dw, conv_ln_g, conv_ln_b,
           conv_w_pw, conv_b_pw, pool_w, pool_scale, sb_bias, mem_g, w_mem_kv, w_out, final_g):
    bp, lp, d = x_prompt.shape
    bs, ls, _ = x_sample.shape
    depth = w_in.shape[0]
    n_mem = mem_prompt.shape[1]
    n_phys = cache_k.shape[1]
    past = page_table.shape[1] * PAGE_SIZE
    w = GROUP_W

    xp = x_prompt.reshape(bp * lp, d)
    xs = x_sample.reshape(bs * ls, d)
    mem = mem_prompt.reshape(bp * n_mem, d)
    ckt = jnp.transpose(cache_k, (0, 1, 3, 4, 2)).reshape(depth, n_phys, w, PAGE_SIZE)
    cvt = jnp.transpose(cache_v, (0, 1, 3, 4, 2)).reshape(depth, n_phys, w, PAGE_SIZE)
    zero_conv = jnp.zeros((bp, CONV_HIST, w), F32)
    zero_pool = jnp.zeros((bp, POOL_HIST, w), F32)
    prompt_tiles = (256, 512, 512, BF16)
    sample_tiles = (bs * ls, ls, ls, F32)

    outs = [[] for _ in range(10)]
    for l in range(depth):
        final = l == depth - 1
        lw = (pre_g[l], post_g[l], w_in[l].astype(BF16), conv_w_dw[l], conv_b_dw[l], conv_ln_g[l],
              conv_ln_b[l], conv_w_pw[l].astype(BF16), conv_b_pw[l], pool_w[l].astype(BF16),
              pool_scale[l], sb_bias[l], w_out[l].astype(BF16))
        mk_p, mv_p = [a.reshape(bp, n_mem, w)
                      for a in rms_proj(mem, mem_g[l], w_mem_kv[l].astype(BF16), 256)]
        xp, c_p, p_p, k_p, v_p = _mixer_layer(
            xp, bp, lp, zero_conv, zero_pool, 0, sb_prompt, mk_p, mv_p, lw, final_g, final,
            prompt_tiles, True)

        def sb_fn(q, k, v, g, bias, l=l):
            return sb_sample(q, k, v, g, bias, ckt, cvt, page_table, l, 8)

        xs, c_s, p_s, k_s, v_s = _mixer_layer(
            xs, bs, ls, state_conv[l], state_pool[l], past, sb_fn,
            cache_mem_k[l].reshape(bs, n_mem, w), cache_mem_v[l].reshape(bs, n_mem, w), lw,
            final_g, final, sample_tiles, False)
        for lst, a in zip(outs, (k_p, v_p, k_s, v_s, c_p, c_s, p_p, p_s, mk_p, mv_p)):
            lst.append(a)

    k_p, v_p, k_s, v_s, c_p, c_s, p_p, p_s, mk_p, mv_p = [jnp.stack(a) for a in outs]
    heads_first = (depth, bp, SB_HEADS, SB_HEAD_DIM, lp)
    return (xp.reshape(bp, lp, d), xs.reshape(bs, ls, d),
            jnp.transpose(k_p.reshape(heads_first), (0, 1, 4, 2, 3)),
            jnp.transpose(v_p.reshape(heads_first), (0, 1, 4, 2, 3)),
            k_s.reshape(depth, bs, ls, SB_HEADS, SB_HEAD_DIM),
            v_s.reshape(depth, bs, ls, SB_HEADS, SB_HEAD_DIM),
            c_p, c_s, p_p, p_s,
            mk_p.reshape(depth, bp, n_mem, MEM_HEADS, MEM_HEAD_DIM),
            mv_p.reshape(depth, bp, n_mem, MEM_HEADS, MEM_HEAD_DIM))
```

```python
import functools

import jax
import jax.numpy as jnp
from jax import lax
from jax.experimental import pallas as pl
from jax.experimental.pallas import tpu as pltpu

F32 = jnp.float32
BF16 = jnp.bfloat16

EPS = 1e-6
GROUP_W = 512
MIX_GROUPS = ("conv_a", "conv_glu_gate", "conv_silu_gate", "pool_u", "pool_gate", "sb_q", "sb_k",
              "sb_v", "sb_gate", "mem_q", "mem_gate")
K_GROUP = MIX_GROUPS.index("sb_k")
V_GROUP = MIX_GROUPS.index("sb_v")
CONV_K = 31
CONV_HIST = CONV_K - 1
POOL_WINDOWS = (2, 4, 8, 16)
POOL_GROUP = 128
POOL_HIST = max(POOL_WINDOWS) - 1
SB_HEADS = 8
SB_HEAD_DIM = 64
MEM_HEADS = 4
MEM_HEAD_DIM = 128
PAGE_SIZE = 128
LANES = 128
SB_PROMPT_BLOCK = 256
VMEM_LIMIT_BYTES = 56 * 1024 * 1024


def _params(*semantics):
    return pltpu.CompilerParams(dimension_semantics=semantics, vmem_limit_bytes=VMEM_LIMIT_BYTES)


def _sigmoid(x):
    return 1.0 / (1.0 + jnp.exp(-x))


def _silu(x):
    return x * _sigmoid(x)


def _rms(x, g):
    return x * lax.rsqrt(jnp.mean(x * x, axis=-1, keepdims=True) + EPS) * g


def _rms_proj_body(x_ref, g_ref, w_ref, *o_refs, plan):
    h = _rms(x_ref[...], g_ref[...]).astype(BF16)
    done = {}
    for (c, kind), o_ref in zip(plan, o_refs):
        if c not in done:
            done = {c: jnp.dot(h, w_ref[:, c * GROUP_W:(c + 1) * GROUP_W],
                               preferred_element_type=F32)}
        y = done[c]
        if kind == "rows":
            o_ref[...] = y.astype(o_ref.dtype)
        else:
            o_ref[0] = y.T


def rms_proj(x, g, w, tm, plan=None, seq_len=None):
    m, d = x.shape
    n = w.shape[1]
    if plan is None:
        plan = tuple((c, "rows", F32) for c in range(n // GROUP_W))
    out_specs, out_shape = [], []
    for _, kind, dtype in plan:
        if kind == "rows":
            out_specs.append(pl.BlockSpec((tm, GROUP_W), lambda i: (i, 0)))
            out_shape.append(jax.ShapeDtypeStruct((m, GROUP_W), dtype))
        else:
            per_seq = seq_len // tm
            out_specs.append(pl.BlockSpec((1, GROUP_W, tm), lambda i: (i // per_seq, 0, i % per_seq)))
            out_shape.append(jax.ShapeDtypeStruct((m // seq_len, GROUP_W, seq_len), dtype))
    return pl.pallas_call(
        functools.partial(_rms_proj_body, plan=tuple((c, kind) for c, kind, _ in plan)),
        grid=(m // tm,),
        in_specs=[pl.BlockSpec((tm, d), lambda i: (i, 0)),
                  pl.BlockSpec((1, d), lambda i: (0, 0)),
                  pl.BlockSpec((d, n), lambda i: (0, 0))],
        out_specs=out_specs,
        out_shape=out_shape,
        compiler_params=_params("parallel"),
        name="rms_proj",
    )(x, g.reshape(1, d), w)


def _conv_body(a_ref, b_ref, g_ref, hist_ref, wdw_ref, bdw_ref, lng_ref, lnb_ref, wpw_ref, bpw_ref,
               y_ref, new_ref, ext_ref, s_ref, *, t, rows):
    ti = pl.program_id(1)
    lead = 32 - CONV_HIST

    @pl.when(ti == 0)
    def _():
        ext_ref[lead:32, :] = hist_ref[0]

    @pl.when(ti > 0)
    def _():
        ext_ref[lead:32, :] = ext_ref[t + lead:t + 32, :]

    ext_ref[32:32 + t, :] = a_ref[0] * _sigmoid(b_ref[0])

    for c in range(t // rows):
        base = c * rows
        acc = jnp.broadcast_to(bdw_ref[...], (rows, GROUP_W))
        for j in range(CONV_K):
            acc = acc + wdw_ref[j:j + 1, :] * ext_ref[base + lead + j:base + lead + j + rows, :]
        mu = jnp.mean(acc, axis=-1, keepdims=True)
        xc = acc - mu
        var = jnp.mean(xc * xc, axis=-1, keepdims=True)
        yn = xc * lax.rsqrt(var + EPS) * lng_ref[...] + lnb_ref[...]
        s_ref[base:base + rows, :] = _silu(yn).astype(BF16)

    y = jnp.dot(s_ref[...], wpw_ref[...], preferred_element_type=F32) + bpw_ref[...]
    y_ref[0] = (y * _silu(g_ref[0])).astype(y_ref.dtype)

    @pl.when(ti == pl.num_programs(1) - 1)
    def _():
        new_ref[0] = ext_ref[t + lead:t + 32, :]


def conv_mixer(a, b, g, hist, w_dw, b_dw, ln_g, ln_b, w_pw, b_pw, t, out_dtype):
    n, l, w = a.shape
    rows = min(t, 32)
    tile = pl.BlockSpec((1, t, w), lambda i, j: (i, j, 0))
    vec = pl.BlockSpec((1, w), lambda i, j: (0, 0))
    return pl.pallas_call(
        functools.partial(_conv_body, t=t, rows=rows),
        grid=(n, l // t),
        in_specs=[tile, tile, tile,
                  pl.BlockSpec((1, CONV_HIST, w), lambda i, j: (i, 0, 0)),
                  pl.BlockSpec((CONV_K, w), lambda i, j: (0, 0)),
                  vec, vec, vec,
                  pl.BlockSpec((w, w), lambda i, j: (0, 0)),
                  vec],
        out_specs=[tile, pl.BlockSpec((1, CONV_HIST, w), lambda i, j: (i, 0, 0))],
        out_shape=[jax.ShapeDtypeStruct((n, l, w), out_dtype),
                   jax.ShapeDtypeStruct((n, CONV_HIST, w), F32)],
        scratch_shapes=[pltpu.VMEM((32 + t, w), F32), pltpu.VMEM((t, w), BF16)],
        compiler_params=_params("parallel", "arbitrary"),
        name="conv_mixer",
    )(a, b, g, hist, w_dw, b_dw.reshape(1, w), ln_g.reshape(1, w), ln_b.reshape(1, w),
      w_pw, b_pw.reshape(1, w))


def _pool_body(u_ref, g_ref, hist_ref, w_ref, scale_ref, y_ref, new_ref, ext_ref, *, t, pos0):
    ti = pl.program_id(1)
    lead = 16 - POOL_HIST

    @pl.when(ti == 0)
    def _():
        ext_ref[lead:16, :] = hist_ref[0]

    @pl.when(ti > 0)
    def _():
        ext_ref[lead:16, :] = ext_ref[t + lead:t + 16, :]

    ext_ref[16:16 + t, :] = u_ref[0]
    pos = pos0 + ti * t + lax.broadcasted_iota(jnp.int32, (t, POOL_GROUP), 0)
    for gi, win in enumerate(POOL_WINDOWS):
        sl = slice(gi * POOL_GROUP, (gi + 1) * POOL_GROUP)
        u = u_ref[0, :, sl]
        tot = u
        for k in range(1, win):
            tot = tot + ext_ref[16 - k:16 - k + t, sl]
        cnt = jnp.minimum(win, pos + 1).astype(F32)
        d = tot / cnt - u
        y = jnp.dot(d.astype(BF16), w_ref[gi], preferred_element_type=F32) * scale_ref[:, sl]
        y_ref[0, :, sl] = (y * _silu(g_ref[0, :, sl])).astype(y_ref.dtype)

    @pl.when(ti == pl.num_programs(1) - 1)
    def _():
        new_ref[0] = ext_ref[t + lead:t + 16, :]


def pool_mixer(u, g, hist, w_pool, scale, t, pos0, out_dtype):
    n, l, w = u.shape
    tile = pl.BlockSpec((1, t, w), lambda i, j: (i, j, 0))
    return pl.pallas_call(
        functools.partial(_pool_body, t=t, pos0=pos0),
        grid=(n, l // t),
        in_specs=[tile, tile,
                  pl.BlockSpec((1, POOL_HIST, w), lambda i, j: (i, 0, 0)),
                  pl.BlockSpec(w_pool.shape, lambda i, j: (0, 0, 0)),
                  pl.BlockSpec((1, w), lambda i, j: (0, 0))],
        out_specs=[tile, pl.BlockSpec((1, POOL_HIST, w), lambda i, j: (i, 0, 0))],
        out_shape=[jax.ShapeDtypeStruct((n, l, w), out_dtype),
                   jax.ShapeDtypeStruct((n, POOL_HIST, w), F32)],
        scratch_shapes=[pltpu.VMEM((16 + t, w), F32)],
        compiler_params=_params("parallel", "arbitrary"),
        name="pool_mixer",
    )(u, g, hist, w_pool, scale.reshape(1, w))


def _suffix_sum_matrix(blk):
    r = lax.broadcasted_iota(jnp.int32, (blk, blk), 0)
    c = lax.broadcasted_iota(jnp.int32, (blk, blk), 1)
    half = jnp.concatenate([(r >= c).astype(BF16), jnp.ones((blk, blk), BF16)], axis=1)
    return jnp.concatenate([half, half], axis=0)


def _sb_block(nz, tri, carry, mask):
    r = jnp.dot(_sb_log_keep(nz, mask), tri, preferred_element_type=F32)
    return _sb_weights(r, carry, nz, mask)


def _sb_log_keep(nz, mask):
    lg = jnp.minimum(nz, 0.0) - jnp.log(1.0 + jnp.exp(-jnp.abs(nz)))
    if mask is not None:
        lg = jnp.where(mask, lg, 0.0)
    hi = lg.astype(BF16)
    lo = (lg - hi.astype(F32)).astype(BF16)
    return jnp.concatenate([hi, lo], axis=1)


def _sb_weights(r, carry, nz, mask):
    blk = nz.shape[1]
    a = jnp.exp(r[:, :blk] + carry - nz)
    if mask is not None:
        a = jnp.where(mask, a, 0.0)
    return a, carry + r[:, blk:]


def _sb_prompt_body(bias_ref, q_ref, kt_ref, v_ref, g_ref, tri_ref, o_ref,
                    qq_ref, nz_ref, hl_ref, r_ref, aa_ref, carry_ref, acc_ref):
    tq = SB_PROMPT_BLOCK
    sub = LANES
    halves = tq // sub
    pairs = SB_HEADS // 2
    i = pl.program_id(1)
    even_q = lax.broadcasted_iota(jnp.int32, (tq, LANES), 1) < SB_HEAD_DIM
    even_v = lax.broadcasted_iota(jnp.int32, (sub, LANES), 1) < SB_HEAD_DIM
    row = lax.broadcasted_iota(jnp.int32, (2 * tq, sub), 0)
    top = row < tq
    col = lax.broadcasted_iota(jnp.int32, (2 * tq, sub), 1)
    qpos = jnp.where(top, row, row - tq)
    zero = jnp.zeros((), BF16)

    for p in range(pairs):
        qp = q_ref[0, :, p * LANES:(p + 1) * LANES] * (-(SB_HEAD_DIM ** -0.5))
        qq_ref[p] = jnp.concatenate([jnp.where(even_q, qp, 0.0), jnp.where(even_q, 0.0, qp)],
                                    axis=0).astype(BF16)
    carry_ref[...] = jnp.zeros_like(carry_ref)
    acc_ref[...] = jnp.zeros_like(acc_ref)

    def key_block(j, diagonal):
        start = pl.multiple_of(j * tq, tq)
        masks = [col + h * sub < qpos if diagonal else None for h in range(halves)]
        for p in range(pairs):
            kt = kt_ref[0, p * LANES:(p + 1) * LANES, pl.ds(start, tq)].astype(BF16)
            nbias = jnp.where(top, -bias_ref[2 * p], -bias_ref[2 * p + 1])
            nz = jnp.dot(qq_ref[p], kt, preferred_element_type=F32)
            for h in range(halves):
                nz_ref[p, :, h * sub:(h + 1) * sub] = nz[:, h * sub:(h + 1) * sub] + nbias
        for p in range(pairs):
            for h in range(halves):
                hl_ref[p, h] = _sb_log_keep(nz_ref[p, :, h * sub:(h + 1) * sub], masks[h])
        for p in range(pairs):
            for h in range(halves):
                r_ref[p, h] = jnp.dot(hl_ref[p, h], tri_ref[...], preferred_element_type=F32)
        for p in range(pairs):
            carry = carry_ref[p]
            for h in reversed(range(halves)):
                a, carry = _sb_weights(r_ref[p, h], carry, nz_ref[p, :, h * sub:(h + 1) * sub],
                                       masks[h])
                ab = a.astype(BF16)
                aa_ref[p, :, 2 * h * sub:(2 * h + 1) * sub] = ab[:tq]
                aa_ref[p, :, (2 * h + 1) * sub:(2 * h + 2) * sub] = ab[tq:]
            carry_ref[p] = carry
        for p in range(pairs):
            vb = v_ref[0, pl.ds(start, tq), p * LANES:(p + 1) * LANES]
            vv = []
            for h in range(halves):
                vh = vb[h * sub:(h + 1) * sub]
                vv += [jnp.where(even_v, vh, zero), jnp.where(even_v, zero, vh)]
            acc_ref[p] += jnp.dot(aa_ref[p], jnp.concatenate(vv, axis=0),
                                  preferred_element_type=F32)

    key_block(i, True)

    def step(it, _):
        key_block(i - 1 - it, False)
        return 0

    lax.fori_loop(0, i, step, 0)
    for p in range(pairs):
        sl = slice(p * LANES, (p + 1) * LANES)
        o_ref[0, :, sl] = (acc_ref[p] * _silu(g_ref[0, :, sl])).astype(o_ref.dtype)


def sb_prompt(q, kt, v, g, bias):
    n, l, w = q.shape
    tq = SB_PROMPT_BLOCK
    sub = LANES
    halves = tq // sub
    pairs = SB_HEADS // 2
    tile = pl.BlockSpec((1, tq, w), lambda b, i: (b, i, 0))
    return pl.pallas_call(
        _sb_prompt_body,
        grid=(n, l // tq),
        in_specs=[pl.BlockSpec(memory_space=pltpu.SMEM), tile,
                  pl.BlockSpec((1, w, l), lambda b, i: (b, 0, 0)),
                  pl.BlockSpec((1, l, w), lambda b, i: (b, 0, 0)), tile,
                  pl.BlockSpec((2 * sub, 2 * sub), lambda b, i: (0, 0))],
        out_specs=tile,
        out_shape=jax.ShapeDtypeStruct((n, l, w), BF16),
        scratch_shapes=[pltpu.VMEM((pairs, 2 * tq, LANES), BF16),
                        pltpu.VMEM((pairs, 2 * tq, tq), F32),
                        pltpu.VMEM((pairs, halves, 2 * tq, 2 * sub), BF16),
                        pltpu.VMEM((pairs, halves, 2 * tq, 2 * sub), F32),
                        pltpu.VMEM((pairs, tq, halves * 2 * sub), BF16),
                        pltpu.VMEM((pairs, 2 * tq, sub), F32),
                        pltpu.VMEM((pairs, tq, LANES), F32)],
        compiler_params=_params("parallel", "arbitrary"),
        name="sb_prompt",
    )(bias, q, kt, v, g, _suffix_sum_matrix(sub))


def _sb_sample_body(pt_ref, q_ref, kn_ref, vn_ref, g_ref, nbias_ref, tri_ref, *rest, ls, pages):
    kt_refs = rest[:pages]
    vt_refs = rest[pages:2 * pages]
    o_ref = rest[2 * pages]
    (qall_ref, carry_ref, acc_ref, pad_ref, ktb_ref, vtb_ref, nz_ref, hl_ref, r_ref,
     a_ref) = rest[2 * pages + 1:]
    blk = PAGE_SIZE
    rows = ls * SB_HEADS
    w = SB_HEADS * SB_HEAD_DIM
    s = pl.program_id(1)
    nt = (((1,), (1,)), ((), ()))
    head_mask = (lax.broadcasted_iota(jnp.int32, (SB_HEADS, w), 1) // SB_HEAD_DIM
                 == lax.broadcasted_iota(jnp.int32, (SB_HEADS, w), 0))
    tri = tri_ref[...]

    def block(nz, weighted_values, mask):
        a, carry = _sb_block(nz + nbias_ref[...], tri, carry_ref[...], mask)
        carry_ref[...] = carry
        acc_ref[...] += weighted_values(a.astype(BF16))

    @pl.when(s == 0)
    def _():
        for t in range(ls):
            qt = q_ref[0, t:t + 1, :] * (-(SB_HEAD_DIM ** -0.5))
            qall_ref[t * SB_HEADS:(t + 1) * SB_HEADS, :] = jnp.where(head_mask, qt, 0.0).astype(BF16)
        carry_ref[...] = jnp.zeros_like(carry_ref)
        acc_ref[...] = jnp.zeros_like(acc_ref)
        key = lax.broadcasted_iota(jnp.int32, (rows, blk), 1)
        qry = lax.broadcasted_iota(jnp.int32, (rows, blk), 0) // SB_HEADS
        pad_ref[...] = jnp.zeros_like(pad_ref)
        pad_ref[0:ls, :] = kn_ref[0]
        nz = lax.dot_general(qall_ref[...], pad_ref[...].astype(BF16), nt, preferred_element_type=F32)
        pad_ref[0:ls, :] = vn_ref[0]
        v_new = pad_ref[...].astype(BF16)
        block(nz, lambda a: jnp.dot(a, v_new, preferred_element_type=F32), key < qry)

    for r in range(pages):
        ktb_ref[:, r * blk:(r + 1) * blk] = kt_refs[r][0, 0].astype(BF16)
        vtb_ref[:, r * blk:(r + 1) * blk] = vt_refs[r][0, 0].astype(BF16)
    nz_ref[...] = (jnp.dot(qall_ref[...], ktb_ref[...], preferred_element_type=F32)
                   + jnp.concatenate([nbias_ref[...]] * pages, axis=1))
    for r in range(pages):
        hl_ref[r * rows:(r + 1) * rows, :] = _sb_log_keep(nz_ref[:, r * blk:(r + 1) * blk], None)
    r_ref[...] = jnp.dot(hl_ref[...], tri, preferred_element_type=F32)
    carry = carry_ref[...]
    for r in range(pages):
        a, carry = _sb_weights(r_ref[r * rows:(r + 1) * rows, :], carry,
                               nz_ref[:, r * blk:(r + 1) * blk], None)
        a_ref[:, r * blk:(r + 1) * blk] = a.astype(BF16)
    carry_ref[...] = carry
    acc_ref[...] += lax.dot_general(a_ref[...], vtb_ref[...], nt, preferred_element_type=F32)

    @pl.when(s == pl.num_programs(1) - 1)
    def _():
        for t in range(ls):
            o = jnp.where(head_mask, acc_ref[t * SB_HEADS:(t + 1) * SB_HEADS, :], 0.0)
            o = jnp.sum(o, axis=0, keepdims=True)
            o_ref[0, t:t + 1, :] = (o * _silu(g_ref[0, t:t + 1, :])).astype(o_ref.dtype)


def sb_sample(q, k_new, v_new, g, bias, cache_kt, cache_vt, page_table, layer, pages):
    n, ls, w = q.shape
    n_pages = page_table.shape[1]
    rows = ls * SB_HEADS
    blk = PAGE_SIZE
    small = pl.BlockSpec((1, ls, w), lambda b, s, pt: (b, 0, 0))

    def page_spec(r):
        def index(b, s, pt):
            return (layer, pt[b * n_pages + n_pages - 1 - (s * pages + r)], 0, 0)
        return pl.BlockSpec((1, 1, w, blk), index)

    nbias_rows = jnp.broadcast_to(-jnp.tile(bias, ls)[:, None], (rows, blk)).astype(F32)
    grid_spec = pltpu.PrefetchScalarGridSpec(
        num_scalar_prefetch=1,
        grid=(n, n_pages // pages),
        in_specs=[small, small, small, small,
                  pl.BlockSpec((rows, blk), lambda b, s, pt: (0, 0)),
                  pl.BlockSpec((2 * blk, 2 * blk), lambda b, s, pt: (0, 0))]
                 + [page_spec(r) for r in range(pages)] * 2,
        out_specs=small,
        scratch_shapes=[pltpu.VMEM((rows, w), BF16), pltpu.VMEM((rows, blk), F32),
                        pltpu.VMEM((rows, w), F32), pltpu.VMEM((blk, w), F32),
                        pltpu.VMEM((w, pages * blk), BF16), pltpu.VMEM((w, pages * blk), BF16),
                        pltpu.VMEM((rows, pages * blk), F32),
                        pltpu.VMEM((pages * rows, 2 * blk), BF16),
                        pltpu.VMEM((pages * rows, 2 * blk), F32),
                        pltpu.VMEM((rows, pages * blk), BF16)],
    )
    return pl.pallas_call(
        functools.partial(_sb_sample_body, ls=ls, pages=pages),
        grid_spec=grid_spec,
        out_shape=jax.ShapeDtypeStruct((n, ls, w), F32),
        compiler_params=_params("parallel", "arbitrary"),
        name="sb_sample",
    )(page_table.reshape(-1), q, k_new, v_new, g, nbias_rows, _suffix_sum_matrix(blk),
      *([cache_kt] * pages), *([cache_vt] * pages))


def _mem_body(q_ref, g_ref, mk_ref, mv_ref, o_ref):
    nt = (((1,), (1,)), ((), ()))
    for h in range(MEM_HEADS):
        sl = slice(h * MEM_HEAD_DIM, (h + 1) * MEM_HEAD_DIM)
        s = lax.dot_general(q_ref[0, :, sl].astype(BF16), mk_ref[0, :, sl].astype(BF16), nt,
                            preferred_element_type=F32) * (MEM_HEAD_DIM ** -0.5)
        e = jnp.exp(s - jnp.max(s, axis=-1, keepdims=True))
        p = e / jnp.sum(e, axis=-1, keepdims=True)
        o = jnp.dot(p.astype(BF16), mv_ref[0, :, sl].astype(BF16), preferred_element_type=F32)
        o_ref[0, :, sl] = (o * _silu(g_ref[0, :, sl])).astype(o_ref.dtype)


def mem_attend(q, g, mk, mv, t, out_dtype):
    n, l, w = q.shape
    n_mem = mk.shape[1]
    tile = pl.BlockSpec((1, t, w), lambda b, i: (b, i, 0))
    mem = pl.BlockSpec((1, n_mem, w), lambda b, i: (b, 0, 0))
    return pl.pallas_call(
        _mem_body,
        grid=(n, l // t),
        in_specs=[tile, tile, mem, mem],
        out_specs=tile,
        out_shape=jax.ShapeDtypeStruct((n, l, w), out_dtype),
        compiler_params=_params("parallel", "arbitrary"),
        name="mem_attend",
    )(q, g, mk, mv)


def _out_proj_body(c_ref, p_ref, s_ref, m_ref, w_ref, pg_ref, x_ref, fg_ref, o_ref, *, final):
    acc = None
    for gi, part in enumerate((c_ref, p_ref, s_ref, m_ref)):
        d = jnp.dot(part[...].astype(BF16), w_ref[gi * GROUP_W:(gi + 1) * GROUP_W, :],
                    preferred_element_type=F32)
        acc = d if acc is None else acc + d
    y = x_ref[...] + _rms(acc, pg_ref[...])
    if final:
        y = _rms(y, fg_ref[...])
    o_ref[...] = y


def out_proj(parts, w, post_g, x, final_g, tm, final):
    m, d = x.shape
    part = pl.BlockSpec((tm, GROUP_W), lambda i: (i, 0))
    vec = pl.BlockSpec((1, d), lambda i: (0, 0))
    row = pl.BlockSpec((tm, d), lambda i: (i, 0))
    return pl.pallas_call(
        functools.partial(_out_proj_body, final=final),
        grid=(m // tm,),
        in_specs=[part, part, part, part, pl.BlockSpec(w.shape, lambda i: (0, 0)), vec, row, vec],
        out_specs=row,
        out_shape=jax.ShapeDtypeStruct((m, d), F32),
        compiler_params=_params("parallel"),
        name="out_proj",
    )(*parts, w, post_g.reshape(1, d), x, final_g.reshape(1, d))


def _mixer_layer(x, n, l, conv_hist, pool_hist, pos0, sb_fn, mk, mv, lw, final_g, final, tiles,
                 keys_on_lanes):
    (pre_g, post_g, w_in, w_dw, b_dw, ln_g, ln_b, w_pw, b_pw, pool_w, pool_scale, sb_bias, w_out) = lw
    tm, t_local, t_mem, mix_dtype = tiles
    plan = [(c, "rows", F32) for c in range(len(MIX_GROUPS))]
    if keys_on_lanes:
        plan[K_GROUP] = (K_GROUP, "cols", F32)
        plan[V_GROUP:V_GROUP + 1] = [(V_GROUP, "cols", F32), (V_GROUP, "rows", BF16)]
    z = list(rms_proj(x, pre_g, w_in, tm, tuple(plan), l))
    v_rows = z.pop(V_GROUP + 1) if keys_on_lanes else z[V_GROUP]
    ca, cb, cg, pu, pg, q, k, v, sg, mq, mg = [
        a if keys_on_lanes and i in (K_GROUP, V_GROUP) else a.reshape(n, l, GROUP_W)
        for i, a in enumerate(z)]
    v_rows = v_rows.reshape(n, l, GROUP_W)
    y_c, conv_new = conv_mixer(ca, cb, cg, conv_hist, w_dw, b_dw, ln_g, ln_b, w_pw, b_pw, t_local,
                               mix_dtype)
    y_p, pool_new = pool_mixer(pu, pg, pool_hist, pool_w, pool_scale, t_local, pos0, mix_dtype)
    y_s = sb_fn(q, k, v_rows, sg, sb_bias)
    y_m = mem_attend(mq, mg, mk, mv, t_mem, mix_dtype)
    parts = [a.reshape(n * l, GROUP_W) for a in (y_c, y_p, y_s, y_m)]
    x = out_proj(parts, w_out, post_g, x, final_g, tm, final)
    return x, conv_new, pool_new, k, v


def kernel(x_prompt, x_sample, mem_prompt, cache_k, cache_v, cache_mem_k, cache_mem_v, state_conv,
           state_pool, page_table, pre_g, post_g, w_in, conv_w_dw, conv_b_dw, conv_ln_g, conv_ln_b,
           conv_w_pw, conv_b_pw, pool_w, pool_scale, sb_bias, mem_g, w_mem_kv, w_out, final_g):
    bp, lp, d = x_prompt.shape
    bs, ls, _ = x_sample.shape
    depth = w_in.shape[0]
    n_mem = mem_prompt.shape[1]
    n_phys = cache_k.shape[1]
    past = page_table.shape[1] * PAGE_SIZE
    w = GROUP_W

    xp = x_prompt.reshape(bp * lp, d)
    xs = x_sample.reshape(bs * ls, d)
    mem = mem_prompt.reshape(bp * n_mem, d)
    ckt = jnp.transpose(cache_k, (0, 1, 3, 4, 2)).reshape(depth, n_phys, w, PAGE_SIZE)
    cvt = jnp.transpose(cache_v, (0, 1, 3, 4, 2)).reshape(depth, n_phys, w, PAGE_SIZE)
    zero_conv = jnp.zeros((bp, CONV_HIST, w), F32)
    zero_pool = jnp.zeros((bp, POOL_HIST, w), F32)
    prompt_tiles = (256, 512, 512, BF16)
    sample_tiles = (bs * ls, ls, ls, F32)

    outs = [[] for _ in range(10)]
    for l in range(depth):
        final = l == depth - 1
        lw = (pre_g[l], post_g[l], w_in[l].astype(BF16), conv_w_dw[l], conv_b_dw[l], conv_ln_g[l],
              conv_ln_b[l], conv_w_pw[l].astype(BF16), conv_b_pw[l], pool_w[l].astype(BF16),
              pool_scale[l], sb_bias[l], w_out[l].astype(BF16))
        mk_p, mv_p = [a.reshape(bp, n_mem, w)
                      for a in rms_proj(mem, mem_g[l], w_mem_kv[l].astype(BF16), 256)]
        xp, c_p, p_p, k_p, v_p = _mixer_layer(
            xp, bp, lp, zero_conv, zero_pool, 0, sb_prompt, mk_p, mv_p, lw, final_g, final,
            prompt_tiles, True)

        def sb_fn(q, k, v, g, bias, l=l):
            return sb_sample(q, k, v, g, bias, ckt, cvt, page_table, l, 8)

        xs, c_s, p_s, k_s, v_s = _mixer_layer(
            xs, bs, ls, state_conv[l], state_pool[l], past, sb_fn,
            cache_mem_k[l].reshape(bs, n_mem, w), cache_mem_v[l].reshape(bs, n_mem, w), lw,
            final_g, final, sample_tiles, False)
        for lst, a in zip(outs, (k_p, v_p, k_s, v_s, c_p, c_s, p_p, p_s, mk_p, mv_p)):
            lst.append(a)

    k_p, v_p, k_s, v_s, c_p, c_s, p_p, p_s, mk_p, mv_p = [jnp.stack(a) for a in outs]
    heads_first = (depth, bp, SB_HEADS, SB_HEAD_DIM, lp)
    return (xp.reshape(bp, lp, d), xs.reshape(bs, ls, d),
            jnp.transpose(k_p.reshape(heads_first), (0, 1, 4, 2, 3)),
            jnp.transpose(v_p.reshape(heads_first), (0, 1, 4, 2, 3)),
            k_s.reshape(depth, bs, ls, SB_HEADS, SB_HEAD_DIM),
            v_s.reshape(depth, bs, ls, SB_HEADS, SB_HEAD_DIM),
            c_p, c_s, p_p, p_s,
            mk_p.reshape(depth, bp, n_mem, MEM_HEADS, MEM_HEAD_DIM),
            mv_p.reshape(depth, bp, n_mem, MEM_HEADS, MEM_HEAD_DIM))
```

```python
import functools

import jax
import jax.numpy as jnp
from jax import lax
from jax.experimental import pallas as pl
from jax.experimental.pallas import tpu as pltpu

F32 = jnp.float32
BF16 = jnp.bfloat16

EPS = 1e-6
GROUP_W = 512
MIX_GROUPS = ("conv_a", "conv_glu_gate", "conv_silu_gate", "pool_u", "pool_gate", "sb_q", "sb_k",
              "sb_v", "sb_gate", "mem_q", "mem_gate")
K_GROUP = MIX_GROUPS.index("sb_k")
V_GROUP = MIX_GROUPS.index("sb_v")
CONV_K = 31
CONV_HIST = CONV_K - 1
POOL_WINDOWS = (2, 4, 8, 16)
POOL_GROUP = 128
POOL_HIST = max(POOL_WINDOWS) - 1
SB_HEADS = 8
SB_HEAD_DIM = 64
MEM_HEADS = 4
MEM_HEAD_DIM = 128
PAGE_SIZE = 128
LANES = 128
SUBLANES = 8
SB_PROMPT_BLOCK = 256
SB_SAMPLE_PAGES = 16
LOG2_E = 1.4426950408889634
VMEM_LIMIT_BYTES = 56 * 1024 * 1024


def _params(*semantics):
    return pltpu.CompilerParams(dimension_semantics=semantics, vmem_limit_bytes=VMEM_LIMIT_BYTES)


def _sigmoid(x):
    return 1.0 / (1.0 + jnp.exp(-x))


def _silu(x):
    return x * _sigmoid(x)


def _rms(x, g):
    return x * lax.rsqrt(jnp.mean(x * x, axis=-1, keepdims=True) + EPS) * g


def _rms_proj_body(x_ref, g_ref, w_ref, *o_refs, plan):
    h = _rms(x_ref[...], g_ref[...]).astype(BF16)
    done = {}
    for (c, kind), o_ref in zip(plan, o_refs):
        if c not in done:
            done = {c: jnp.dot(h, w_ref[:, c * GROUP_W:(c + 1) * GROUP_W],
                               preferred_element_type=F32)}
        y = done[c]
        if kind == "rows":
            o_ref[...] = y.astype(o_ref.dtype)
        else:
            o_ref[0] = y.T


def rms_proj(x, g, w, tm, plan=None, seq_len=None):
    m, d = x.shape
    n = w.shape[1]
    if plan is None:
        plan = tuple((c, "rows", F32) for c in range(n // GROUP_W))
    out_specs, out_shape = [], []
    for _, kind, dtype in plan:
        if kind == "rows":
            out_specs.append(pl.BlockSpec((tm, GROUP_W), lambda i: (i, 0)))
            out_shape.append(jax.ShapeDtypeStruct((m, GROUP_W), dtype))
        else:
            per_seq = seq_len // tm
            out_specs.append(pl.BlockSpec((1, GROUP_W, tm), lambda i: (i // per_seq, 0, i % per_seq)))
            out_shape.append(jax.ShapeDtypeStruct((m // seq_len, GROUP_W, seq_len), dtype))
    return pl.pallas_call(
        functools.partial(_rms_proj_body, plan=tuple((c, kind) for c, kind, _ in plan)),
        grid=(m // tm,),
        in_specs=[pl.BlockSpec((tm, d), lambda i: (i, 0)),
                  pl.BlockSpec((1, d), lambda i: (0, 0)),
                  pl.BlockSpec((d, n), lambda i: (0, 0), pipeline_mode=pl.Buffered(1))],
        out_specs=out_specs,
        out_shape=out_shape,
        compiler_params=_params("parallel"),
        name="rms_proj",
    )(x, g.reshape(1, d), w)


def _conv_body(a_ref, b_ref, g_ref, hist_ref, wdw_ref, bdw_ref, lng_ref, lnb_ref, wpw_ref, bpw_ref,
               y_ref, new_ref, shift_ref, s_ref, taps_ref, *, t, rows):
    ext_ref = shift_ref.at[0]
    ti = pl.program_id(1)
    lead = 32 - CONV_HIST
    n_ext = shift_ref.shape[1]

    @pl.when(ti == 0)
    def _():
        ext_ref[...] = jnp.zeros((n_ext, GROUP_W), F32)
        ext_ref[lead:32, :] = hist_ref[0]

    @pl.when(ti > 0)
    def _():
        ext_ref[lead:32, :] = ext_ref[t + lead:t + 32, :]

    ext_ref[32:32 + t, :] = a_ref[0] * _sigmoid(b_ref[0])
    ext = ext_ref[...]
    for r in range(1, SUBLANES):
        shift_ref[r] = pltpu.roll(ext, n_ext - r, axis=0)

    tile_rows = min(rows, SUBLANES)
    for j in range(CONV_K):
        taps_ref[j] = jnp.broadcast_to(wdw_ref[j:j + 1, :], (SUBLANES, GROUP_W))
    split = (rows // tile_rows, tile_rows, GROUP_W)

    for c in range(t // rows):
        base = c * rows
        acc = jnp.broadcast_to(bdw_ref[...], (rows, GROUP_W)).reshape(split)
        for j in range(CONV_K):
            blocks, r = divmod(lead + j, SUBLANES)
            start = base + blocks * SUBLANES
            acc = acc + (taps_ref[j, 0:tile_rows, :][None]
                         * shift_ref[r, start:start + rows, :].reshape(split))
        acc = acc.reshape(rows, GROUP_W)
        mu = jnp.mean(acc, axis=-1, keepdims=True)
        xc = acc - mu
        var = jnp.mean(xc * xc, axis=-1, keepdims=True)
        yn = xc * lax.rsqrt(var + EPS) * lng_ref[...] + lnb_ref[...]
        s_ref[base:base + rows, :] = _silu(yn).astype(BF16)

    y = jnp.dot(s_ref[...], wpw_ref[...], preferred_element_type=F32) + bpw_ref[...]
    y_ref[0] = (y * _silu(g_ref[0])).astype(y_ref.dtype)

    @pl.when(ti == pl.num_programs(1) - 1)
    def _():
        new_ref[0] = ext_ref[t + lead:t + 32, :]


def conv_mixer(a, b, g, hist, w_dw, b_dw, ln_g, ln_b, w_pw, b_pw, t, out_dtype):
    n, l, w = a.shape
    rows = min(t, 32)
    tile = pl.BlockSpec((1, t, w), lambda i, j: (i, j, 0))
    vec = pl.BlockSpec((1, w), lambda i, j: (0, 0))
    return pl.pallas_call(
        functools.partial(_conv_body, t=t, rows=rows),
        grid=(n, l // t),
        in_specs=[tile, tile, tile,
                  pl.BlockSpec((1, CONV_HIST, w), lambda i, j: (i, 0, 0)),
                  pl.BlockSpec((CONV_K, w), lambda i, j: (0, 0)),
                  vec, vec, vec,
                  pl.BlockSpec((w, w), lambda i, j: (0, 0)),
                  vec],
        out_specs=[tile, pl.BlockSpec((1, CONV_HIST, w), lambda i, j: (i, 0, 0))],
        out_shape=[jax.ShapeDtypeStruct((n, l, w), out_dtype),
                   jax.ShapeDtypeStruct((n, CONV_HIST, w), F32)],
        scratch_shapes=[pltpu.VMEM((SUBLANES, pl.cdiv(32 + t, SUBLANES) * SUBLANES, w), F32),
                        pltpu.VMEM((t, w), BF16), pltpu.VMEM((CONV_K, SUBLANES, w), F32)],
        compiler_params=_params("parallel", "arbitrary"),
        name="conv_mixer",
    )(a, b, g, hist, w_dw, b_dw.reshape(1, w), ln_g.reshape(1, w), ln_b.reshape(1, w),
      w_pw, b_pw.reshape(1, w))


def _pool_body(u_ref, g_ref, hist_ref, w_ref, scale_ref, y_ref, new_ref, ext_ref, *, t, pos0):
    ti = pl.program_id(1)
    lead = 16 - POOL_HIST

    @pl.when(ti == 0)
    def _():
        ext_ref[lead:16, :] = hist_ref[0]

    @pl.when(ti > 0)
    def _():
        ext_ref[lead:16, :] = ext_ref[t + lead:t + 16, :]

    ext_ref[16:16 + t, :] = u_ref[0]
    pos = pos0 + ti * t + lax.broadcasted_iota(jnp.int32, (t, POOL_GROUP), 0)
    for gi, win in enumerate(POOL_WINDOWS):
        sl = slice(gi * POOL_GROUP, (gi + 1) * POOL_GROUP)
        u = u_ref[0, :, sl]
        tot = u
        for k in range(1, win):
            tot = tot + ext_ref[16 - k:16 - k + t, sl]
        cnt = jnp.minimum(win, pos + 1).astype(F32)
        d = tot / cnt - u
        y = jnp.dot(d.astype(BF16), w_ref[gi], preferred_element_type=F32) * scale_ref[:, sl]
        y_ref[0, :, sl] = (y * _silu(g_ref[0, :, sl])).astype(y_ref.dtype)

    @pl.when(ti == pl.num_programs(1) - 1)
    def _():
        new_ref[0] = ext_ref[t + lead:t + 16, :]


def pool_mixer(u, g, hist, w_pool, scale, t, pos0, out_dtype):
    n, l, w = u.shape
    tile = pl.BlockSpec((1, t, w), lambda i, j: (i, j, 0))
    return pl.pallas_call(
        functools.partial(_pool_body, t=t, pos0=pos0),
        grid=(n, l // t),
        in_specs=[tile, tile,
                  pl.BlockSpec((1, POOL_HIST, w), lambda i, j: (i, 0, 0)),
                  pl.BlockSpec(w_pool.shape, lambda i, j: (0, 0, 0)),
                  pl.BlockSpec((1, w), lambda i, j: (0, 0))],
        out_specs=[tile, pl.BlockSpec((1, POOL_HIST, w), lambda i, j: (i, 0, 0))],
        out_shape=[jax.ShapeDtypeStruct((n, l, w), out_dtype),
                   jax.ShapeDtypeStruct((n, POOL_HIST, w), F32)],
        scratch_shapes=[pltpu.VMEM((16 + t, w), F32)],
        compiler_params=_params("parallel", "arbitrary"),
        name="pool_mixer",
    )(u, g, hist, w_pool, scale.reshape(1, w))


def _suffix_sum_matrix(blk):
    r = lax.broadcasted_iota(jnp.int32, (blk, blk), 0)
    c = lax.broadcasted_iota(jnp.int32, (blk, blk), 1)
    half = jnp.concatenate([(r >= c).astype(BF16), jnp.ones((blk, blk), BF16)], axis=1)
    return jnp.concatenate([half, half], axis=0)


def _sb_block(nz, tri, carry, mask):
    r = jnp.dot(_sb_log_keep(nz, mask), tri, preferred_element_type=F32)
    return _sb_weights(r, carry, nz, mask)


def _sb_log_keep(nz, mask):
    lg = jnp.minimum(nz, 0.0) - jnp.log(1.0 + jnp.exp2(jnp.abs(nz) * (-LOG2_E)))
    if mask is not None:
        lg = jnp.where(mask, lg, 0.0)
    hi = lg.astype(BF16)
    lo = (lg - hi.astype(F32)).astype(BF16)
    return jnp.concatenate([hi, lo], axis=1)


def _sb_weights(r, carry, nz, mask):
    blk = nz.shape[1]
    a = jnp.exp(r[:, :blk] + carry - nz)
    if mask is not None:
        a = jnp.where(mask, a, 0.0)
    return a, carry + r[:, blk:]


def _sb_prompt_body(bias_ref, q_ref, kt_ref, v_ref, g_ref, tri_ref, o_ref,
                    qq_ref, nz_ref, hl_ref, r_ref, aa_ref, carry_ref, acc_ref):
    tq = SB_PROMPT_BLOCK
    sub = LANES
    halves = tq // sub
    pairs = SB_HEADS // 2
    i = pl.program_id(1)
    even_q = lax.broadcasted_iota(jnp.int32, (tq, LANES), 1) < SB_HEAD_DIM
    even_v = lax.broadcasted_iota(jnp.int32, (sub, LANES), 1) < SB_HEAD_DIM
    row = lax.broadcasted_iota(jnp.int32, (2 * tq, sub), 0)
    top = row < tq
    col = lax.broadcasted_iota(jnp.int32, (2 * tq, sub), 1)
    qpos = jnp.where(top, row, row - tq)
    zero = jnp.zeros((), BF16)

    for p in range(pairs):
        qp = q_ref[0, :, p * LANES:(p + 1) * LANES] * (-(SB_HEAD_DIM ** -0.5))
        qq_ref[p] = jnp.concatenate([jnp.where(even_q, qp, 0.0), jnp.where(even_q, 0.0, qp)],
                                    axis=0).astype(BF16)
    carry_ref[...] = jnp.zeros_like(carry_ref)
    acc_ref[...] = jnp.zeros_like(acc_ref)

    def key_block(j, diagonal):
        start = pl.multiple_of(j * tq, tq)
        masks = [col + h * sub < qpos if diagonal else None for h in range(halves)]
        for p in range(pairs):
            kt = kt_ref[0, p * LANES:(p + 1) * LANES, pl.ds(start, tq)].astype(BF16)
            nbias = jnp.where(top, -bias_ref[2 * p], -bias_ref[2 * p + 1])
            nz = jnp.dot(qq_ref[p], kt, preferred_element_type=F32)
            for h in range(halves):
                nz_ref[p, :, h * sub:(h + 1) * sub] = nz[:, h * sub:(h + 1) * sub] + nbias
        for p in range(pairs):
            for h in range(halves):
                hl_ref[p, h] = _sb_log_keep(nz_ref[p, :, h * sub:(h + 1) * sub], masks[h])
        for p in range(pairs):
            for h in range(halves):
                r_ref[p, h] = jnp.dot(hl_ref[p, h], tri_ref[...], preferred_element_type=F32)
        for p in range(pairs):
            carry = carry_ref[p]
            for h in reversed(range(halves)):
                a, carry = _sb_weights(r_ref[p, h], carry, nz_ref[p, :, h * sub:(h + 1) * sub],
                                       masks[h])
                ab = a.astype(BF16)
                aa_ref[p, :, 2 * h * sub:(2 * h + 1) * sub] = ab[:tq]
                aa_ref[p, :, (2 * h + 1) * sub:(2 * h + 2) * sub] = ab[tq:]
            carry_ref[p] = carry
        for p in range(pairs):
            vb = v_ref[0, pl.ds(start, tq), p * LANES:(p + 1) * LANES]
            vv = []
            for h in range(halves):
                vh = vb[h * sub:(h + 1) * sub]
                vv += [jnp.where(even_v, vh, zero), jnp.where(even_v, zero, vh)]
            acc_ref[p] += jnp.dot(aa_ref[p], jnp.concatenate(vv, axis=0),
                                  preferred_element_type=F32)

    key_block(i, True)

    def step(it, _):
        key_block(i - 1 - it, False)
        return 0

    lax.fori_loop(0, i, step, 0)
    for p in range(pairs):
        sl = slice(p * LANES, (p + 1) * LANES)
        o_ref[0, :, sl] = (acc_ref[p] * _silu(g_ref[0, :, sl])).astype(o_ref.dtype)


def sb_prompt(q, kt, v, g, bias):
    n, l, w = q.shape
    tq = SB_PROMPT_BLOCK
    sub = LANES
    halves = tq // sub
    pairs = SB_HEADS // 2
    tile = pl.BlockSpec((1, tq, w), lambda b, i: (b, i, 0))
    return pl.pallas_call(
        _sb_prompt_body,
        grid=(n, l // tq),
        in_specs=[pl.BlockSpec(memory_space=pltpu.SMEM), tile,
                  pl.BlockSpec((1, w, l), lambda b, i: (b, 0, 0)),
                  pl.BlockSpec((1, l, w), lambda b, i: (b, 0, 0)), tile,
                  pl.BlockSpec((2 * sub, 2 * sub), lambda b, i: (0, 0))],
        out_specs=tile,
        out_shape=jax.ShapeDtypeStruct((n, l, w), BF16),
        scratch_shapes=[pltpu.VMEM((pairs, 2 * tq, LANES), BF16),
                        pltpu.VMEM((pairs, 2 * tq, tq), F32),
                        pltpu.VMEM((pairs, halves, 2 * tq, 2 * sub), BF16),
                        pltpu.VMEM((pairs, halves, 2 * tq, 2 * sub), F32),
                        pltpu.VMEM((pairs, tq, halves * 2 * sub), BF16),
                        pltpu.VMEM((pairs, 2 * tq, sub), F32),
                        pltpu.VMEM((pairs, tq, LANES), F32)],
        compiler_params=_params("parallel", "arbitrary"),
        name="sb_prompt",
    )(bias, q, kt, v, g, _suffix_sum_matrix(sub))


def _sb_sample_body(pt_ref, q_ref, kn_ref, vn_ref, g_ref, nbias_ref, tri_ref, *rest, ls, pages):
    kt_refs = rest[:pages]
    vt_refs = rest[pages:2 * pages]
    o_ref = rest[2 * pages]
    (qall_ref, carry_ref, acc_ref, pad_ref, ktb_ref, vtb_ref, nz_ref, hl_ref, r_ref,
     a_ref) = rest[2 * pages + 1:]
    blk = PAGE_SIZE
    rows = ls * SB_HEADS
    w = SB_HEADS * SB_HEAD_DIM
    s = pl.program_id(1)
    nt = (((1,), (1,)), ((), ()))
    head_mask = (lax.broadcasted_iota(jnp.int32, (SB_HEADS, w), 1) // SB_HEAD_DIM
                 == lax.broadcasted_iota(jnp.int32, (SB_HEADS, w), 0))
    tri = tri_ref[...]

    def block(nz, weighted_values, mask):
        a, carry = _sb_block(nz + nbias_ref[...], tri, carry_ref[...], mask)
        carry_ref[...] = carry
        acc_ref[...] += weighted_values(a.astype(BF16))

    @pl.when(s == 0)
    def _():
        for t in range(ls):
            qt = q_ref[0, t:t + 1, :] * (-(SB_HEAD_DIM ** -0.5))
            qall_ref[t * SB_HEADS:(t + 1) * SB_HEADS, :] = jnp.where(head_mask, qt, 0.0).astype(BF16)
        carry_ref[...] = jnp.zeros_like(carry_ref)
        acc_ref[...] = jnp.zeros_like(acc_ref)
        key = lax.broadcasted_iota(jnp.int32, (rows, blk), 1)
        qry = lax.broadcasted_iota(jnp.int32, (rows, blk), 0) // SB_HEADS
        pad_ref[...] = jnp.zeros_like(pad_ref)
        pad_ref[0:ls, :] = kn_ref[0]
        nz = lax.dot_general(qall_ref[...], pad_ref[...].astype(BF16), nt, preferred_element_type=F32)
        pad_ref[0:ls, :] = vn_ref[0]
        v_new = pad_ref[...].astype(BF16)
        block(nz, lambda a: jnp.dot(a, v_new, preferred_element_type=F32), key < qry)

    for r in range(pages):
        ktb_ref[:, r * blk:(r + 1) * blk] = kt_refs[r][0, 0].astype(BF16)
        vtb_ref[:, r * blk:(r + 1) * blk] = vt_refs[r][0, 0].astype(BF16)
    nz_ref[...] = (jnp.dot(qall_ref[...], ktb_ref[...], preferred_element_type=F32)
                   + jnp.concatenate([nbias_ref[...]] * pages, axis=1))
    for r in range(pages):
        hl_ref[r * rows:(r + 1) * rows, :] = _sb_log_keep(nz_ref[:, r * blk:(r + 1) * blk], None)
    r_ref[...] = jnp.dot(hl_ref[...], tri, preferred_element_type=F32)
    carry = carry_ref[...]
    for r in range(pages):
        a, carry = _sb_weights(r_ref[r * rows:(r + 1) * rows, :], carry,
                               nz_ref[:, r * blk:(r + 1) * blk], None)
        a_ref[:, r * blk:(r + 1) * blk] = a.astype(BF16)
    carry_ref[...] = carry
    acc_ref[...] += lax.dot_general(a_ref[...], vtb_ref[...], nt, preferred_element_type=F32)

    @pl.when(s == pl.num_programs(1) - 1)
    def _():
        for t in range(ls):
            o = jnp.where(head_mask, acc_ref[t * SB_HEADS:(t + 1) * SB_HEADS, :], 0.0)
            o = jnp.sum(o, axis=0, keepdims=True)
            o_ref[0, t:t + 1, :] = (o * _silu(g_ref[0, t:t + 1, :])).astype(o_ref.dtype)


def sb_sample(q, k_new, v_new, g, bias, cache_kt, cache_vt, page_table, layer, pages):
    n, ls, w = q.shape
    n_pages = page_table.shape[1]
    rows = ls * SB_HEADS
    blk = PAGE_SIZE
    small = pl.BlockSpec((1, ls, w), lambda b, s, pt: (b, 0, 0))

    def page_spec(r):
        def index(b, s, pt):
            return (layer, pt[b * n_pages + n_pages - 1 - (s * pages + r)], 0, 0)
        return pl.BlockSpec((1, 1, w, blk), index)

    nbias_rows = jnp.broadcast_to(-jnp.tile(bias, ls)[:, None], (rows, blk)).astype(F32)
    grid_spec = pltpu.PrefetchScalarGridSpec(
        num_scalar_prefetch=1,
        grid=(n, n_pages // pages),
        in_specs=[small, small, small, small,
                  pl.BlockSpec((rows, blk), lambda b, s, pt: (0, 0)),
                  pl.BlockSpec((2 * blk, 2 * blk), lambda b, s, pt: (0, 0))]
                 + [page_spec(r) for r in range(pages)] * 2,
        out_specs=small,
        scratch_shapes=[pltpu.VMEM((rows, w), BF16), pltpu.VMEM((rows, blk), F32),
                        pltpu.VMEM((rows, w), F32), pltpu.VMEM((blk, w), F32),
                        pltpu.VMEM((w, pages * blk), BF16), pltpu.VMEM((w, pages * blk), BF16),
                        pltpu.VMEM((rows, pages * blk), F32),
                        pltpu.VMEM((pages * rows, 2 * blk), BF16),
                        pltpu.VMEM((pages * rows, 2 * blk), F32),
                        pltpu.VMEM((rows, pages * blk), BF16)],
    )
    return pl.pallas_call(
        functools.partial(_sb_sample_body, ls=ls, pages=pages),
        grid_spec=grid_spec,
        out_shape=jax.ShapeDtypeStruct((n, ls, w), F32),
        compiler_params=_params("parallel", "arbitrary"),
        name="sb_sample",
    )(page_table.reshape(-1), q, k_new, v_new, g, nbias_rows, _suffix_sum_matrix(blk),
      *([cache_kt] * pages), *([cache_vt] * pages))


def _mem_body(q_ref, g_ref, mk_ref, mv_ref, o_ref):
    nt = (((1,), (1,)), ((), ()))
    for h in range(MEM_HEADS):
        sl = slice(h * MEM_HEAD_DIM, (h + 1) * MEM_HEAD_DIM)
        s = lax.dot_general(q_ref[0, :, sl].astype(BF16), mk_ref[0, :, sl].astype(BF16), nt,
                            preferred_element_type=F32) * (MEM_HEAD_DIM ** -0.5)
        e = jnp.exp(s - jnp.max(s, axis=-1, keepdims=True))
        p = e / jnp.sum(e, axis=-1, keepdims=True)
        o = jnp.dot(p.astype(BF16), mv_ref[0, :, sl].astype(BF16), preferred_element_type=F32)
        o_ref[0, :, sl] = (o * _silu(g_ref[0, :, sl])).astype(o_ref.dtype)


def mem_attend(q, g, mk, mv, t, out_dtype):
    n, l, w = q.shape
    n_mem = mk.shape[1]
    tile = pl.BlockSpec((1, t, w), lambda b, i: (b, i, 0))
    mem = pl.BlockSpec((1, n_mem, w), lambda b, i: (b, 0, 0))
    return pl.pallas_call(
        _mem_body,
        grid=(n, l // t),
        in_specs=[tile, tile, mem, mem],
        out_specs=tile,
        out_shape=jax.ShapeDtypeStruct((n, l, w), out_dtype),
        compiler_params=_params("parallel", "arbitrary"),
        name="mem_attend",
    )(q, g, mk, mv)


def _out_proj_body(c_ref, p_ref, s_ref, m_ref, w_ref, pg_ref, x_ref, fg_ref, o_ref, *, final):
    acc = None
    for gi, part in enumerate((c_ref, p_ref, s_ref, m_ref)):
        d = jnp.dot(part[...].astype(BF16), w_ref[gi * GROUP_W:(gi + 1) * GROUP_W, :],
                    preferred_element_type=F32)
        acc = d if acc is None else acc + d
    y = x_ref[...] + _rms(acc, pg_ref[...])
    if final:
        y = _rms(y, fg_ref[...])
    o_ref[...] = y


def out_proj(parts, w, post_g, x, final_g, tm, final):
    m, d = x.shape
    part = pl.BlockSpec((tm, GROUP_W), lambda i: (i, 0))
    vec = pl.BlockSpec((1, d), lambda i: (0, 0))
    row = pl.BlockSpec((tm, d), lambda i: (i, 0))
    return pl.pallas_call(
        functools.partial(_out_proj_body, final=final),
        grid=(m // tm,),
        in_specs=[part, part, part, part, pl.BlockSpec(w.shape, lambda i: (0, 0)), vec, row, vec],
        out_specs=row,
        out_shape=jax.ShapeDtypeStruct((m, d), F32),
        compiler_params=_params("parallel"),
        name="out_proj",
    )(*parts, w, post_g.reshape(1, d), x, final_g.reshape(1, d))


def _mixer_layer(x, n, l, conv_hist, pool_hist, pos0, sb_fn, mk, mv, lw, final_g, final, tiles,
                 keys_on_lanes):
    (pre_g, post_g, w_in, w_dw, b_dw, ln_g, ln_b, w_pw, b_pw, pool_w, pool_scale, sb_bias, w_out) = lw
    tm, tm_out, t_local, t_mem, mix_dtype = tiles
    plan = [(c, "rows", F32) for c in range(len(MIX_GROUPS))]
    if keys_on_lanes:
        plan[K_GROUP] = (K_GROUP, "cols", F32)
        plan[V_GROUP:V_GROUP + 1] = [(V_GROUP, "cols", F32), (V_GROUP, "rows", BF16)]
    z = list(rms_proj(x, pre_g, w_in, tm, tuple(plan), l))
    v_rows = z.pop(V_GROUP + 1) if keys_on_lanes else z[V_GROUP]
    ca, cb, cg, pu, pg, q, k, v, sg, mq, mg = [
        a if keys_on_lanes and i in (K_GROUP, V_GROUP) else a.reshape(n, l, GROUP_W)
        for i, a in enumerate(z)]
    v_rows = v_rows.reshape(n, l, GROUP_W)
    y_c, conv_new = conv_mixer(ca, cb, cg, conv_hist, w_dw, b_dw, ln_g, ln_b, w_pw, b_pw, t_local,
                               mix_dtype)
    y_p, pool_new = pool_mixer(pu, pg, pool_hist, pool_w, pool_scale, t_local, pos0, mix_dtype)
    y_s = sb_fn(q, k, v_rows, sg, sb_bias)
    y_m = mem_attend(mq, mg, mk, mv, t_mem, mix_dtype)
    parts = [a.reshape(n * l, GROUP_W) for a in (y_c, y_p, y_s, y_m)]
    x = out_proj(parts, w_out, post_g, x, final_g, tm_out, final)
    return x, conv_new, pool_new, k, v


def kernel(x_prompt, x_sample, mem_prompt, cache_k, cache_v, cache_mem_k, cache_mem_v, state_conv,
           state_pool, page_table, pre_g, post_g, w_in, conv_w_dw, conv_b_dw, conv_ln_g, conv_ln_b,
           conv_w_pw, conv_b_pw, pool_w, pool_scale, sb_bias, mem_g, w_mem_kv, w_out, final_g):
    bp, lp, d = x_prompt.shape
    bs, ls, _ = x_sample.shape
    depth = w_in.shape[0]
    n_mem = mem_prompt.shape[1]
    n_phys = cache_k.shape[1]
    past = page_table.shape[1] * PAGE_SIZE
    w = GROUP_W

    xp = x_prompt.reshape(bp * lp, d)
    xs = x_sample.reshape(bs * ls, d)
    mem = mem_prompt.reshape(bp * n_mem, d)
    ckt = jnp.transpose(cache_k, (0, 1, 3, 4, 2)).reshape(depth, n_phys, w, PAGE_SIZE)
    cvt = jnp.transpose(cache_v, (0, 1, 3, 4, 2)).reshape(depth, n_phys, w, PAGE_SIZE)
    zero_conv = jnp.zeros((bp, CONV_HIST, w), F32)
    zero_pool = jnp.zeros((bp, POOL_HIST, w), F32)
    prompt_tiles = (512, 512, 512, 512, BF16)
    sample_tiles = (bs * ls, bs * ls, ls, ls, F32)

    outs = [[] for _ in range(10)]
    for l in range(depth):
        final = l == depth - 1
        lw = (pre_g[l], post_g[l], w_in[l].astype(BF16), conv_w_dw[l], conv_b_dw[l], conv_ln_g[l],
              conv_ln_b[l], conv_w_pw[l].astype(BF16), conv_b_pw[l], pool_w[l].astype(BF16),
              pool_scale[l], sb_bias[l], w_out[l].astype(BF16))
        mk_p, mv_p = [a.reshape(bp, n_mem, w)
                      for a in rms_proj(mem, mem_g[l], w_mem_kv[l].astype(BF16), 256)]
        xp, c_p, p_p, k_p, v_p = _mixer_layer(
            xp, bp, lp, zero_conv, zero_pool, 0, sb_prompt, mk_p, mv_p, lw, final_g, final,
            prompt_tiles, True)

        def sb_fn(q, k, v, g, bias, l=l):
            return sb_sample(q, k, v, g, bias, ckt, cvt, page_table, l, SB_SAMPLE_PAGES)

        xs, c_s, p_s, k_s, v_s = _mixer_layer(
            xs, bs, ls, state_conv[l], state_pool[l], past, sb_fn,
            cache_mem_k[l].reshape(bs, n_mem, w), cache_mem_v[l].reshape(bs, n_mem, w), lw,
            final_g, final, sample_tiles, False)
        for lst, a in zip(outs, (k_p, v_p, k_s, v_s, c_p, c_s, p_p, p_s, mk_p, mv_p)):
            lst.append(a)

    k_p, v_p, k_s, v_s, c_p, c_s, p_p, p_s, mk_p, mv_p = [jnp.stack(a) for a in outs]
    heads_first = (depth, bp, SB_HEADS, SB_HEAD_DIM, lp)
    return (xp.reshape(bp, lp, d), xs.reshape(bs, ls, d),
            jnp.transpose(k_p.reshape(heads_first), (0, 1, 4, 2, 3)),
            jnp.transpose(v_p.reshape(heads_first), (0, 1, 4, 2, 3)),
            k_s.reshape(depth, bs, ls, SB_HEADS, SB_HEAD_DIM),
            v_s.reshape(depth, bs, ls, SB_HEADS, SB_HEAD_DIM),
            c_p, c_s, p_p, p_s,
            mk_p.reshape(depth, bp, n_mem, MEM_HEADS, MEM_HEAD_DIM),
            mv_p.reshape(depth, bp, n_mem, MEM_HEADS, MEM_HEAD_DIM))
```

```python
import functools

import jax
import jax.numpy as jnp
from jax import lax
from jax.experimental import pallas as pl
from jax.experimental.pallas import tpu as pltpu

F32 = jnp.float32
BF16 = jnp.bfloat16

EPS = 1e-6
GROUP_W = 512
MIX_GROUPS = ("conv_a", "conv_glu_gate", "conv_silu_gate", "pool_u", "pool_gate", "sb_q", "sb_k",
              "sb_v", "sb_gate", "mem_q", "mem_gate")
K_GROUP = MIX_GROUPS.index("sb_k")
V_GROUP = MIX_GROUPS.index("sb_v")
CONV_K = 31
CONV_HIST = CONV_K - 1
POOL_WINDOWS = (2, 4, 8, 16)
POOL_GROUP = 128
POOL_HIST = max(POOL_WINDOWS) - 1
SB_HEADS = 8
SB_HEAD_DIM = 64
MEM_HEADS = 4
MEM_HEAD_DIM = 128
PAGE_SIZE = 128
LANES = 128
SUBLANES = 8
SB_PROMPT_BLOCK = 256
SB_SAMPLE_PAGES = 8
SB_SAMPLE_RING = 3
LOG2_E = 1.4426950408889634
VMEM_LIMIT_BYTES = 56 * 1024 * 1024


def _params(*semantics):
    return pltpu.CompilerParams(dimension_semantics=semantics, vmem_limit_bytes=VMEM_LIMIT_BYTES)


def _sigmoid(x):
    return 1.0 / (1.0 + jnp.exp(-x))


def _silu(x):
    return x * _sigmoid(x)


def _rms(x, g):
    return x * lax.rsqrt(jnp.mean(x * x, axis=-1, keepdims=True) + EPS) * g


def _rms_proj_body(x_ref, g_ref, w_ref, *o_refs, plan):
    h = _rms(x_ref[...], g_ref[...]).astype(BF16)
    done = {}
    for (c, kind), o_ref in zip(plan, o_refs):
        if c not in done:
            done = {c: jnp.dot(h, w_ref[:, c * GROUP_W:(c + 1) * GROUP_W],
                               preferred_element_type=F32)}
        y = done[c]
        if kind == "rows":
            o_ref[...] = y.astype(o_ref.dtype)
        else:
            o_ref[0] = y.T


def rms_proj(x, g, w, tm, plan=None, seq_len=None):
    m, d = x.shape
    n = w.shape[1]
    if plan is None:
        plan = tuple((c, "rows", F32) for c in range(n // GROUP_W))
    out_specs, out_shape = [], []
    for _, kind, dtype in plan:
        if kind == "rows":
            out_specs.append(pl.BlockSpec((tm, GROUP_W), lambda i: (i, 0)))
            out_shape.append(jax.ShapeDtypeStruct((m, GROUP_W), dtype))
        else:
            per_seq = seq_len // tm
            out_specs.append(pl.BlockSpec((1, GROUP_W, tm), lambda i: (i // per_seq, 0, i % per_seq)))
            out_shape.append(jax.ShapeDtypeStruct((m // seq_len, GROUP_W, seq_len), dtype))
    return pl.pallas_call(
        functools.partial(_rms_proj_body, plan=tuple((c, kind) for c, kind, _ in plan)),
        grid=(m // tm,),
        in_specs=[pl.BlockSpec((tm, d), lambda i: (i, 0)),
                  pl.BlockSpec((1, d), lambda i: (0, 0)),
                  pl.BlockSpec((d, n), lambda i: (0, 0), pipeline_mode=pl.Buffered(1))],
        out_specs=out_specs,
        out_shape=out_shape,
        compiler_params=_params("parallel"),
        name="rms_proj",
    )(x, g.reshape(1, d), w)


def _conv_body(a_ref, b_ref, g_ref, hist_ref, wdw_ref, bdw_ref, lng_ref, lnb_ref, wpw_ref, bpw_ref,
               y_ref, new_ref, shift_ref, s_ref, taps_ref, *, t, rows, nb):
    ext_ref = shift_ref.at[0]
    ti = pl.program_id(1)
    lead = 32 - CONV_HIST
    n_ext = shift_ref.shape[1]
    tile_rows = min(rows, SUBLANES)
    split = (rows // tile_rows, tile_rows, GROUP_W)

    for bi in range(nb):
        seq = slice(bi * t, (bi + 1) * t)

        @pl.when(ti == 0)
        def _():
            ext_ref[...] = jnp.zeros((n_ext, GROUP_W), F32)
            ext_ref[lead:32, :] = hist_ref[bi]

        @pl.when(ti > 0)
        def _():
            ext_ref[lead:32, :] = ext_ref[t + lead:t + 32, :]

        ext_ref[32:32 + t, :] = a_ref[seq, :] * _sigmoid(b_ref[seq, :])
        ext = ext_ref[...]
        for r in range(1, SUBLANES):
            shift_ref[r] = pltpu.roll(ext, n_ext - r, axis=0)
        for j in range(CONV_K):
            taps_ref[j] = jnp.broadcast_to(wdw_ref[j:j + 1, :], (SUBLANES, GROUP_W))

        new_ref[bi] = ext_ref[t + lead:t + 32, :]

        for c in range(t // rows):
            base = c * rows
            acc = jnp.broadcast_to(bdw_ref[...], (rows, GROUP_W)).reshape(split)
            for j in range(CONV_K):
                blocks, r = divmod(lead + j, SUBLANES)
                start = base + blocks * SUBLANES
                acc = acc + (taps_ref[j, 0:tile_rows, :][None]
                             * shift_ref[r, start:start + rows, :].reshape(split))
            acc = acc.reshape(rows, GROUP_W)
            mu = jnp.mean(acc, axis=-1, keepdims=True)
            xc = acc - mu
            var = jnp.mean(xc * xc, axis=-1, keepdims=True)
            yn = xc * lax.rsqrt(var + EPS) * lng_ref[...] + lnb_ref[...]
            s_ref[bi * t + base:bi * t + base + rows, :] = _silu(yn).astype(BF16)

    y = jnp.dot(s_ref[...], wpw_ref[...], preferred_element_type=F32) + bpw_ref[...]
    y_ref[...] = (y * _silu(g_ref[...])).astype(y_ref.dtype)


def conv_mixer(a, b, g, hist, w_dw, b_dw, ln_g, ln_b, w_pw, b_pw, n, l, t, nb, out_dtype):
    w = a.shape[1]
    rows = min(t, 32)
    tiles = l // t
    tile = pl.BlockSpec((nb * t, w), lambda i, j: (i * tiles + j, 0))
    state = pl.BlockSpec((nb, CONV_HIST, w), lambda i, j: (i, 0, 0))
    vec = pl.BlockSpec((1, w), lambda i, j: (0, 0))
    return pl.pallas_call(
        functools.partial(_conv_body, t=t, rows=rows, nb=nb),
        grid=(n // nb, tiles),
        in_specs=[tile, tile, tile, state,
                  pl.BlockSpec((CONV_K, w), lambda i, j: (0, 0)),
                  vec, vec, vec,
                  pl.BlockSpec((w, w), lambda i, j: (0, 0)),
                  vec],
        out_specs=[tile, state],
        out_shape=[jax.ShapeDtypeStruct((n * l, w), out_dtype),
                   jax.ShapeDtypeStruct((n, CONV_HIST, w), F32)],
        scratch_shapes=[pltpu.VMEM((SUBLANES, pl.cdiv(32 + t, SUBLANES) * SUBLANES, w), F32),
                        pltpu.VMEM((nb * t, w), BF16), pltpu.VMEM((CONV_K, SUBLANES, w), F32)],
        compiler_params=_params("parallel", "arbitrary"),
        name="conv_mixer",
    )(a, b, g, hist, w_dw, b_dw.reshape(1, w), ln_g.reshape(1, w), ln_b.reshape(1, w),
      w_pw, b_pw.reshape(1, w))


def _pool_body(u_ref, g_ref, hist_ref, w_ref, scale_ref, y_ref, new_ref, ext_ref, d_ref,
               *, t, pos0, nb):
    ti = pl.program_id(1)
    lead = 16 - POOL_HIST
    pos = pos0 + ti * t + lax.broadcasted_iota(jnp.int32, (t, POOL_GROUP), 0)

    for bi in range(nb):
        seq = slice(bi * t, (bi + 1) * t)

        @pl.when(ti == 0)
        def _():
            ext_ref[lead:16, :] = hist_ref[bi]

        @pl.when(ti > 0)
        def _():
            ext_ref[lead:16, :] = ext_ref[t + lead:t + 16, :]

        ext_ref[16:16 + t, :] = u_ref[seq, :]
        new_ref[bi] = ext_ref[t + lead:t + 16, :]
        for gi, win in enumerate(POOL_WINDOWS):
            sl = slice(gi * POOL_GROUP, (gi + 1) * POOL_GROUP)
            u = u_ref[seq, sl]
            tot = u
            for k in range(1, win):
                tot = tot + ext_ref[16 - k:16 - k + t, sl]
            cnt = jnp.minimum(win, pos + 1).astype(F32)
            d_ref[seq, sl] = (tot / cnt - u).astype(BF16)

    for gi in range(len(POOL_WINDOWS)):
        sl = slice(gi * POOL_GROUP, (gi + 1) * POOL_GROUP)
        y = jnp.dot(d_ref[:, sl], w_ref[gi], preferred_element_type=F32) * scale_ref[:, sl]
        y_ref[:, sl] = (y * _silu(g_ref[:, sl])).astype(y_ref.dtype)


def pool_mixer(u, g, hist, w_pool, scale, n, l, t, nb, pos0, out_dtype):
    w = u.shape[1]
    tiles = l // t
    tile = pl.BlockSpec((nb * t, w), lambda i, j: (i * tiles + j, 0))
    state = pl.BlockSpec((nb, POOL_HIST, w), lambda i, j: (i, 0, 0))
    return pl.pallas_call(
        functools.partial(_pool_body, t=t, pos0=pos0, nb=nb),
        grid=(n // nb, tiles),
        in_specs=[tile, tile, state,
                  pl.BlockSpec(w_pool.shape, lambda i, j: (0, 0, 0)),
                  pl.BlockSpec((1, w), lambda i, j: (0, 0))],
        out_specs=[tile, state],
        out_shape=[jax.ShapeDtypeStruct((n * l, w), out_dtype),
                   jax.ShapeDtypeStruct((n, POOL_HIST, w), F32)],
        scratch_shapes=[pltpu.VMEM((16 + t, w), F32), pltpu.VMEM((nb * t, w), BF16)],
        compiler_params=_params("parallel", "arbitrary"),
        name="pool_mixer",
    )(u, g, hist, w_pool, scale.reshape(1, w))


def _suffix_sum_matrix(blk):
    r = lax.broadcasted_iota(jnp.int32, (blk, blk), 0)
    c = lax.broadcasted_iota(jnp.int32, (blk, blk), 1)
    half = jnp.concatenate([(r >= c).astype(BF16), jnp.ones((blk, blk), BF16)], axis=1)
    return jnp.concatenate([half, half], axis=0)


def _sb_block(nz, tri, carry, mask):
    r = jnp.dot(_sb_log_keep(nz, mask), tri, preferred_element_type=F32)
    return _sb_weights(r, carry, nz, mask)


def _sb_log_keep(nz, mask):
    lg = jnp.minimum(nz, 0.0) - jnp.log(1.0 + jnp.exp2(jnp.abs(nz) * (-LOG2_E)))
    if mask is not None:
        lg = jnp.where(mask, lg, 0.0)
    hi = lg.astype(BF16)
    lo = (lg - hi.astype(F32)).astype(BF16)
    return jnp.concatenate([hi, lo], axis=1)


def _sb_weights(r, carry, nz, mask):
    blk = nz.shape[1]
    a = jnp.exp(r[:, :blk] + carry - nz)
    if mask is not None:
        a = jnp.where(mask, a, 0.0)
    return a, carry + r[:, blk:]


def _sb_prompt_body(bias_ref, q_ref, kt_ref, v_ref, g_ref, tri_ref, o_ref,
                    qq_ref, nz_ref, hl_ref, r_ref, aa_ref, carry_ref, acc_ref):
    tq = SB_PROMPT_BLOCK
    sub = LANES
    halves = tq // sub
    pairs = SB_HEADS // 2
    i = pl.program_id(1)
    even_q = lax.broadcasted_iota(jnp.int32, (tq, LANES), 1) < SB_HEAD_DIM
    even_v = lax.broadcasted_iota(jnp.int32, (sub, LANES), 1) < SB_HEAD_DIM
    row = lax.broadcasted_iota(jnp.int32, (2 * tq, sub), 0)
    top = row < tq
    col = lax.broadcasted_iota(jnp.int32, (2 * tq, sub), 1)
    qpos = jnp.where(top, row, row - tq)
    zero = jnp.zeros((), BF16)

    for p in range(pairs):
        qp = q_ref[:, p * LANES:(p + 1) * LANES] * (-(SB_HEAD_DIM ** -0.5))
        qq_ref[p] = jnp.concatenate([jnp.where(even_q, qp, 0.0), jnp.where(even_q, 0.0, qp)],
                                    axis=0).astype(BF16)
    carry_ref[...] = jnp.zeros_like(carry_ref)
    acc_ref[...] = jnp.zeros_like(acc_ref)

    def key_block(j, diagonal):
        start = pl.multiple_of(j * tq, tq)
        masks = [col + h * sub < qpos if diagonal else None for h in range(halves)]
        for p in range(pairs):
            kt = kt_ref[0, p * LANES:(p + 1) * LANES, pl.ds(start, tq)].astype(BF16)
            nbias = jnp.where(top, -bias_ref[2 * p], -bias_ref[2 * p + 1])
            nz = jnp.dot(qq_ref[p], kt, preferred_element_type=F32)
            for h in range(halves):
                nz_ref[p, :, h * sub:(h + 1) * sub] = nz[:, h * sub:(h + 1) * sub] + nbias
        for p in range(pairs):
            for h in range(halves):
                hl_ref[p, h] = _sb_log_keep(nz_ref[p, :, h * sub:(h + 1) * sub], masks[h])
        for p in range(pairs):
            for h in range(halves):
                r_ref[p, h] = jnp.dot(hl_ref[p, h], tri_ref[...], preferred_element_type=F32)
        for p in range(pairs):
            carry = carry_ref[p]
            for h in reversed(range(halves)):
                a, carry = _sb_weights(r_ref[p, h], carry, nz_ref[p, :, h * sub:(h + 1) * sub],
                                       masks[h])
                ab = a.astype(BF16)
                aa_ref[p, :, 2 * h * sub:(2 * h + 1) * sub] = ab[:tq]
                aa_ref[p, :, (2 * h + 1) * sub:(2 * h + 2) * sub] = ab[tq:]
            carry_ref[p] = carry
        for p in range(pairs):
            vb = v_ref[pl.ds(start, tq), p * LANES:(p + 1) * LANES]
            vv = []
            for h in range(halves):
                vh = vb[h * sub:(h + 1) * sub]
                vv += [jnp.where(even_v, vh, zero), jnp.where(even_v, zero, vh)]
            acc_ref[p] += jnp.dot(aa_ref[p], jnp.concatenate(vv, axis=0),
                                  preferred_element_type=F32)

    key_block(i, True)

    def step(it, _):
        key_block(i - 1 - it, False)
        return 0

    lax.fori_loop(0, i, step, 0)
    for p in range(pairs):
        sl = slice(p * LANES, (p + 1) * LANES)
        o_ref[:, sl] = (acc_ref[p] * _silu(g_ref[:, sl])).astype(o_ref.dtype)


def sb_prompt(q, kt, v, g, bias):
    n, w, l = kt.shape
    tq = SB_PROMPT_BLOCK
    sub = LANES
    halves = tq // sub
    pairs = SB_HEADS // 2
    tiles = l // tq
    tile = pl.BlockSpec((tq, w), lambda b, i: (b * tiles + i, 0))
    return pl.pallas_call(
        _sb_prompt_body,
        grid=(n, tiles),
        in_specs=[pl.BlockSpec(memory_space=pltpu.SMEM), tile,
                  pl.BlockSpec((1, w, l), lambda b, i: (b, 0, 0)),
                  pl.BlockSpec((l, w), lambda b, i: (b, 0)), tile,
                  pl.BlockSpec((2 * sub, 2 * sub), lambda b, i: (0, 0))],
        out_specs=tile,
        out_shape=jax.ShapeDtypeStruct((n * l, w), BF16),
        scratch_shapes=[pltpu.VMEM((pairs, 2 * tq, LANES), BF16),
                        pltpu.VMEM((pairs, 2 * tq, tq), F32),
                        pltpu.VMEM((pairs, halves, 2 * tq, 2 * sub), BF16),
                        pltpu.VMEM((pairs, halves, 2 * tq, 2 * sub), F32),
                        pltpu.VMEM((pairs, tq, halves * 2 * sub), BF16),
                        pltpu.VMEM((pairs, 2 * tq, sub), F32),
                        pltpu.VMEM((pairs, tq, LANES), F32)],
        compiler_params=_params("parallel", "arbitrary"),
        name="sb_prompt",
    )(bias, q, kt, v, g, _suffix_sum_matrix(sub))


def _sb_sample_body(pt_ref, q_ref, kn_ref, vn_ref, g_ref, nbias_ref, tri_ref, ckt_hbm, cvt_hbm, o_ref,
                    kbuf_ref, vbuf_ref, sem_ref, qall_ref, carry_ref, acc_ref, pad_ref, ktb_ref,
                    vtb_ref, nz_ref, hl_ref, r_ref, a_ref, *, ls, pages, n_pages, layer):
    blk = PAGE_SIZE
    rows = ls * SB_HEADS
    w = SB_HEADS * SB_HEAD_DIM
    b = pl.program_id(0)
    s = pl.program_id(1)
    steps = pl.num_programs(1)
    total = pl.num_programs(0) * steps
    step = b * steps + s
    nt = (((1,), (1,)), ((), ()))
    head_mask = (lax.broadcasted_iota(jnp.int32, (SB_HEADS, w), 1) // SB_HEAD_DIM
                 == lax.broadcasted_iota(jnp.int32, (SB_HEADS, w), 0))
    tri = tri_ref[...]

    def page_copies(at):
        seq = at // steps
        group = at % steps
        slot = at % SB_SAMPLE_RING
        copies = []
        for r in range(pages):
            page = pt_ref[seq * n_pages + n_pages - 1 - (group * pages + r)]
            copies.append(pltpu.make_async_copy(ckt_hbm.at[layer, page], kbuf_ref.at[slot, r],
                                                sem_ref.at[slot, r]))
            copies.append(pltpu.make_async_copy(cvt_hbm.at[layer, page], vbuf_ref.at[slot, r],
                                                sem_ref.at[slot, pages + r]))
        return copies

    @pl.when(step == 0)
    def _():
        for ahead in range(SB_SAMPLE_RING - 1):
            for copy in page_copies(ahead):
                copy.start()

    @pl.when(step + (SB_SAMPLE_RING - 1) < total)
    def _():
        for copy in page_copies(step + (SB_SAMPLE_RING - 1)):
            copy.start()

    @pl.when(s == 0)
    def _():
        for t in range(ls):
            qt = q_ref[pl.ds(b * ls + t, 1), :] * (-(SB_HEAD_DIM ** -0.5))
            qall_ref[t * SB_HEADS:(t + 1) * SB_HEADS, :] = jnp.where(head_mask, qt, 0.0).astype(BF16)
        key = lax.broadcasted_iota(jnp.int32, (rows, blk), 1)
        qry = lax.broadcasted_iota(jnp.int32, (rows, blk), 0) // SB_HEADS
        pad_ref[...] = jnp.zeros_like(pad_ref)
        for t in range(ls):
            pad_ref[t:t + 1, :] = kn_ref[pl.ds(b * ls + t, 1), :]
        nz = lax.dot_general(qall_ref[...], pad_ref[...].astype(BF16), nt, preferred_element_type=F32)
        for t in range(ls):
            pad_ref[t:t + 1, :] = vn_ref[pl.ds(b * ls + t, 1), :]
        a, carry = _sb_block(nz + nbias_ref[...], tri, jnp.zeros((rows, blk), F32), key < qry)
        carry_ref[...] = carry
        acc_ref[...] = jnp.dot(a.astype(BF16), pad_ref[...].astype(BF16), preferred_element_type=F32)

    for copy in page_copies(step):
        copy.wait()
    slot = step % SB_SAMPLE_RING

    for r in range(pages):
        ktb_ref[:, r * blk:(r + 1) * blk] = kbuf_ref[slot, r].astype(BF16)
        vtb_ref[:, r * blk:(r + 1) * blk] = vbuf_ref[slot, r].astype(BF16)
    nz_ref[...] = (jnp.dot(qall_ref[...], ktb_ref[...], preferred_element_type=F32)
                   + jnp.concatenate([nbias_ref[...]] * pages, axis=1))
    for r in range(pages):
        hl_ref[r * rows:(r + 1) * rows, :] = _sb_log_keep(nz_ref[:, r * blk:(r + 1) * blk], None)
    r_ref[...] = jnp.dot(hl_ref[...], tri, preferred_element_type=F32)
    carry = carry_ref[...]
    for r in range(pages):
        a, carry = _sb_weights(r_ref[r * rows:(r + 1) * rows, :], carry,
                               nz_ref[:, r * blk:(r + 1) * blk], None)
        a_ref[:, r * blk:(r + 1) * blk] = a.astype(BF16)
    carry_ref[...] = carry
    acc_ref[...] += lax.dot_general(a_ref[...], vtb_ref[...], nt, preferred_element_type=F32)

    @pl.when(s == steps - 1)
    def _():
        for t in range(ls):
            o = jnp.where(head_mask, acc_ref[t * SB_HEADS:(t + 1) * SB_HEADS, :], 0.0)
            o = jnp.sum(o, axis=0, keepdims=True)
            gate = g_ref[pl.ds(b * ls + t, 1), :]
            o_ref[pl.ds(b * ls + t, 1), :] = (o * _silu(gate)).astype(o_ref.dtype)


def sb_sample(q, k_new, v_new, g, bias, cache_kt, cache_vt, page_table, layer, ls):
    m, w = q.shape
    n, n_pages = page_table.shape
    pages = SB_SAMPLE_PAGES
    rows = ls * SB_HEADS
    blk = PAGE_SIZE
    whole = pl.BlockSpec((m, w), lambda b, s, pt: (0, 0))
    nbias_rows = jnp.broadcast_to(-jnp.tile(bias, ls)[:, None], (rows, blk)).astype(F32)
    grid_spec = pltpu.PrefetchScalarGridSpec(
        num_scalar_prefetch=1,
        grid=(n, n_pages // pages),
        in_specs=[whole, whole, whole, whole,
                  pl.BlockSpec((rows, blk), lambda b, s, pt: (0, 0)),
                  pl.BlockSpec((2 * blk, 2 * blk), lambda b, s, pt: (0, 0)),
                  pl.BlockSpec(memory_space=pl.ANY), pl.BlockSpec(memory_space=pl.ANY)],
        out_specs=whole,
        scratch_shapes=[pltpu.VMEM((SB_SAMPLE_RING, pages, w, blk), F32),
                        pltpu.VMEM((SB_SAMPLE_RING, pages, w, blk), F32),
                        pltpu.SemaphoreType.DMA((SB_SAMPLE_RING, 2 * pages)),
                        pltpu.VMEM((rows, w), BF16), pltpu.VMEM((rows, blk), F32),
                        pltpu.VMEM((rows, w), F32), pltpu.VMEM((blk, w), F32),
                        pltpu.VMEM((w, pages * blk), BF16), pltpu.VMEM((w, pages * blk), BF16),
                        pltpu.VMEM((rows, pages * blk), F32),
                        pltpu.VMEM((pages * rows, 2 * blk), BF16),
                        pltpu.VMEM((pages * rows, 2 * blk), F32),
                        pltpu.VMEM((rows, pages * blk), BF16)],
    )
    return pl.pallas_call(
        functools.partial(_sb_sample_body, ls=ls, pages=pages, n_pages=n_pages, layer=layer),
        grid_spec=grid_spec,
        out_shape=jax.ShapeDtypeStruct((m, w), F32),
        compiler_params=_params("arbitrary", "arbitrary"),
        name="sb_sample",
    )(page_table.reshape(-1), q, k_new, v_new, g, nbias_rows, _suffix_sum_matrix(blk),
      cache_kt, cache_vt)


def _softmax_rows(s):
    e = jnp.exp(s - jnp.max(s, axis=-1, keepdims=True))
    return e / jnp.sum(e, axis=-1, keepdims=True)


def _mem_body(q_ref, g_ref, mk_ref, mv_ref, o_ref, *, t, nb):
    nt = (((1,), (1,)), ((), ()))
    for bi in range(nb):
        seq = slice(bi * t, (bi + 1) * t)
        for h in range(MEM_HEADS):
            sl = slice(h * MEM_HEAD_DIM, (h + 1) * MEM_HEAD_DIM)
            s = lax.dot_general(q_ref[seq, sl].astype(BF16), mk_ref[bi, :, sl].astype(BF16), nt,
                                preferred_element_type=F32) * (MEM_HEAD_DIM ** -0.5)
            o = jnp.dot(_softmax_rows(s).astype(BF16), mv_ref[bi, :, sl].astype(BF16),
                        preferred_element_type=F32)
            o_ref[seq, sl] = (o * _silu(g_ref[seq, sl])).astype(o_ref.dtype)


def _mem_paired_body(q_ref, g_ref, mk_ref, mv_ref, o_ref, *, t, nb):
    nt = (((1,), (1,)), ((), ()))
    n_rows = mk_ref.shape[2] * mk_ref.shape[3]
    row_head = lax.broadcasted_iota(jnp.int32, (MEM_HEADS * t, n_rows), 1) % MEM_HEADS
    q_head = lax.broadcasted_iota(jnp.int32, (MEM_HEADS * t, n_rows), 0) // t
    for bi in range(nb):
        seq = slice(bi * t, (bi + 1) * t)
        heads = [slice(h * MEM_HEAD_DIM, (h + 1) * MEM_HEAD_DIM) for h in range(MEM_HEADS)]
        q = jnp.concatenate([q_ref[seq, sl] for sl in heads], axis=0).astype(BF16)
        mk = mk_ref[0, bi].reshape(n_rows, MEM_HEAD_DIM).astype(BF16)
        mv = mv_ref[0, bi].reshape(n_rows, MEM_HEAD_DIM).astype(BF16)
        s = lax.dot_general(q, mk, nt, preferred_element_type=F32) * (MEM_HEAD_DIM ** -0.5)
        p = _softmax_rows(jnp.where(row_head == q_head, s, -jnp.inf))
        o = jnp.dot(p.astype(BF16), mv, preferred_element_type=F32)
        for h, sl in enumerate(heads):
            o_ref[seq, sl] = (o[h * t:(h + 1) * t] * _silu(g_ref[seq, sl])).astype(o_ref.dtype)


def mem_attend(q, g, mk, mv, n, l, t, nb, out_dtype, layer=None):
    w = q.shape[1]
    tiles = l // t
    tile = pl.BlockSpec((nb * t, w), lambda b, i: (b * tiles + i, 0))
    if layer is None:
        mem = pl.BlockSpec((nb,) + mk.shape[1:], lambda b, i: (b, 0, 0))
    else:
        mem = pl.BlockSpec((1, nb) + mk.shape[2:], lambda b, i: (layer, b, 0, 0, 0))
    return pl.pallas_call(
        functools.partial(_mem_body if layer is None else _mem_paired_body, t=t, nb=nb),
        grid=(n // nb, tiles),
        in_specs=[tile, tile, mem, mem],
        out_specs=tile,
        out_shape=jax.ShapeDtypeStruct((n * l, w), out_dtype),
        compiler_params=_params("parallel", "arbitrary"),
        name="mem_attend",
    )(q, g, mk, mv)


def _out_proj_body(c_ref, p_ref, s_ref, m_ref, w_ref, pg_ref, x_ref, fg_ref, o_ref, *, final):
    acc = None
    for gi, part in enumerate((c_ref, p_ref, s_ref, m_ref)):
        d = jnp.dot(part[...].astype(BF16), w_ref[gi * GROUP_W:(gi + 1) * GROUP_W, :],
                    preferred_element_type=F32)
        acc = d if acc is None else acc + d
    y = x_ref[...] + _rms(acc, pg_ref[...])
    if final:
        y = _rms(y, fg_ref[...])
    o_ref[...] = y


def out_proj(parts, w, post_g, x, final_g, tm, final):
    m, d = x.shape
    part = pl.BlockSpec((tm, GROUP_W), lambda i: (i, 0))
    vec = pl.BlockSpec((1, d), lambda i: (0, 0))
    row = pl.BlockSpec((tm, d), lambda i: (i, 0))
    return pl.pallas_call(
        functools.partial(_out_proj_body, final=final),
        grid=(m // tm,),
        in_specs=[part, part, part, part, pl.BlockSpec(w.shape, lambda i: (0, 0)), vec, row, vec],
        out_specs=row,
        out_shape=jax.ShapeDtypeStruct((m, d), F32),
        compiler_params=_params("parallel"),
        name="out_proj",
    )(*parts, w, post_g.reshape(1, d), x, final_g.reshape(1, d))


def _mixer_layer(x, n, l, conv_hist, pool_hist, pos0, sb_fn, mem_fn, lw, final_g, final, tiles,
                 keys_on_lanes):
    (pre_g, post_g, w_in, w_dw, b_dw, ln_g, ln_b, w_pw, b_pw, pool_w, pool_scale, sb_bias, w_out) = lw
    tm, tm_out, t_local, nb, mix_dtype = tiles
    plan = [(c, "rows", F32) for c in range(len(MIX_GROUPS))]
    if keys_on_lanes:
        plan[K_GROUP] = (K_GROUP, "cols", F32)
        plan[V_GROUP:V_GROUP + 1] = [(V_GROUP, "cols", F32), (V_GROUP, "rows", BF16)]
    z = list(rms_proj(x, pre_g, w_in, tm, tuple(plan), l))
    v_rows = z.pop(V_GROUP + 1) if keys_on_lanes else z[V_GROUP]
    ca, cb, cg, pu, pg, q, k, v, sg, mq, mg = z
    y_c, conv_new = conv_mixer(ca, cb, cg, conv_hist, w_dw, b_dw, ln_g, ln_b, w_pw, b_pw, n, l,
                               t_local, nb, mix_dtype)
    y_p, pool_new = pool_mixer(pu, pg, pool_hist, pool_w, pool_scale, n, l, t_local, nb, pos0,
                               mix_dtype)
    y_s = sb_fn(q, k, v_rows, sg, sb_bias)
    y_m = mem_fn(mq, mg, t_local, nb, mix_dtype)
    x = out_proj([y_c, y_p, y_s, y_m], w_out, post_g, x, final_g, tm_out, final)
    return x, conv_new, pool_new, k, v


def kernel(x_prompt, x_sample, mem_prompt, cache_k, cache_v, cache_mem_k, cache_mem_v, state_conv,
           state_pool, page_table, pre_g, post_g, w_in, conv_w_dw, conv_b_dw, conv_ln_g, conv_ln_b,
           conv_w_pw, conv_b_pw, pool_w, pool_scale, sb_bias, mem_g, w_mem_kv, w_out, final_g):
    bp, lp, d = x_prompt.shape
    bs, ls, _ = x_sample.shape
    depth = w_in.shape[0]
    n_mem = mem_prompt.shape[1]
    n_phys = cache_k.shape[1]
    past = page_table.shape[1] * PAGE_SIZE
    w = GROUP_W

    xp = x_prompt.reshape(bp * lp, d)
    xs = x_sample.reshape(bs * ls, d)
    mem = mem_prompt.reshape(bp * n_mem, d)
    ckt = jnp.transpose(cache_k, (0, 1, 3, 4, 2)).reshape(depth, n_phys, w, PAGE_SIZE)
    cvt = jnp.transpose(cache_v, (0, 1, 3, 4, 2)).reshape(depth, n_phys, w, PAGE_SIZE)
    pair_shape = (depth, bs, n_mem // 2, 2 * MEM_HEADS, MEM_HEAD_DIM)
    cmk = cache_mem_k.reshape(pair_shape)
    cmv = cache_mem_v.reshape(pair_shape)
    zero_conv = jnp.zeros((bp, CONV_HIST, w), F32)
    zero_pool = jnp.zeros((bp, POOL_HIST, w), F32)
    prompt_tiles = (512, 512, 512, 1, BF16)
    sample_tiles = (bs * ls, bs * ls, ls, 8, F32)

    outs = [[] for _ in range(10)]
    for l in range(depth):
        final = l == depth - 1
        lw = (pre_g[l], post_g[l], w_in[l].astype(BF16), conv_w_dw[l], conv_b_dw[l], conv_ln_g[l],
              conv_ln_b[l], conv_w_pw[l].astype(BF16), conv_b_pw[l], pool_w[l].astype(BF16),
              pool_scale[l], sb_bias[l], w_out[l].astype(BF16))
        mk_p, mv_p = [a.reshape(bp, n_mem, w)
                      for a in rms_proj(mem, mem_g[l], w_mem_kv[l].astype(BF16), 256)]

        def mem_prompt_fn(q, g, t, nb, dtype, mk_p=mk_p, mv_p=mv_p):
            return mem_attend(q, g, mk_p, mv_p, bp, lp, t, nb, dtype)

        xp, c_p, p_p, k_p, v_p = _mixer_layer(
            xp, bp, lp, zero_conv, zero_pool, 0, sb_prompt, mem_prompt_fn, lw, final_g, final,
            prompt_tiles, True)

        def sb_sample_fn(q, k, v, g, bias, l=l):
            return sb_sample(q, k, v, g, bias, ckt, cvt, page_table, l, ls)

        def mem_sample_fn(q, g, t, nb, dtype, l=l):
            return mem_attend(q, g, cmk, cmv, bs, ls, t, nb, dtype, layer=l)

        xs, c_s, p_s, k_s, v_s = _mixer_layer(
            xs, bs, ls, state_conv[l], state_pool[l], past, sb_sample_fn, mem_sample_fn, lw,
            final_g, final, sample_tiles, False)
        for lst, a in zip(outs, (k_p, v_p, k_s, v_s, c_p, c_s, p_p, p_s, mk_p, mv_p)):
            lst.append(a)

    k_p, v_p, k_s, v_s, c_p, c_s, p_p, p_s, mk_p, mv_p = [jnp.stack(a) for a in outs]
    heads_first = (depth, bp, SB_HEADS, SB_HEAD_DIM, lp)
    return (xp.reshape(bp, lp, d), xs.reshape(bs, ls, d),
            jnp.transpose(k_p.reshape(heads_first), (0, 1, 4, 2, 3)),
            jnp.transpose(v_p.reshape(heads_first), (0, 1, 4, 2, 3)),
            k_s.reshape(depth, bs, ls, SB_HEADS, SB_HEAD_DIM),
            v_s.reshape(depth, bs, ls, SB_HEADS, SB_HEAD_DIM),
            c_p, c_s, p_p, p_s,
            mk_p.reshape(depth, bp, n_mem, MEM_HEADS, MEM_HEAD_DIM),
            mv_p.reshape(depth, bp, n_mem, MEM_HEADS, MEM_HEAD_DIM))
```

```python
import functools

import jax
import jax.numpy as jnp
from jax import lax
from jax.experimental import pallas as pl
from jax.experimental.pallas import tpu as pltpu

F32 = jnp.float32
BF16 = jnp.bfloat16

EPS = 1e-6
GROUP_W = 512
MIX_GROUPS = ("conv_a", "conv_glu_gate", "conv_silu_gate", "pool_u", "pool_gate", "sb_q", "sb_k",
              "sb_v", "sb_gate", "mem_q", "mem_gate")
K_GROUP = MIX_GROUPS.index("sb_k")
V_GROUP = MIX_GROUPS.index("sb_v")
CONV_K = 31
CONV_HIST = CONV_K - 1
POOL_WINDOWS = (2, 4, 8, 16)
POOL_GROUP = 128
POOL_HIST = max(POOL_WINDOWS) - 1
SB_HEADS = 8
SB_HEAD_DIM = 64
MEM_HEADS = 4
MEM_HEAD_DIM = 128
PAGE_SIZE = 128
LANES = 128
SUBLANES = 8
SB_PROMPT_BLOCK = 256
SB_SAMPLE_PAGES = 8
SB_SAMPLE_RING = 3
LOG2_E = 1.4426950408889634
VMEM_LIMIT_BYTES = 56 * 1024 * 1024


def _params(*semantics):
    return pltpu.CompilerParams(dimension_semantics=semantics, vmem_limit_bytes=VMEM_LIMIT_BYTES)


def _sigmoid(x):
    return 1.0 / (1.0 + jnp.exp(-x))


def _silu(x):
    return x * _sigmoid(x)


def _rms(x, g):
    return x * lax.rsqrt(jnp.mean(x * x, axis=-1, keepdims=True) + EPS) * g


def _rms_proj_body(x_ref, g_ref, w_ref, *o_refs, plan):
    h = _rms(x_ref[...], g_ref[...]).astype(BF16)
    done = {}
    for (c, kind), o_ref in zip(plan, o_refs):
        if c not in done:
            done = {c: jnp.dot(h, w_ref[:, c * GROUP_W:(c + 1) * GROUP_W],
                               preferred_element_type=F32)}
        y = done[c]
        if kind == "rows":
            o_ref[...] = y.astype(o_ref.dtype)
        else:
            o_ref[0] = y.T


def rms_proj(x, g, w, tm, plan=None, seq_len=None):
    m, d = x.shape
    n = w.shape[1]
    if plan is None:
        plan = tuple((c, "rows", F32) for c in range(n // GROUP_W))
    out_specs, out_shape = [], []
    for _, kind, dtype in plan:
        if kind == "rows":
            out_specs.append(pl.BlockSpec((tm, GROUP_W), lambda i: (i, 0)))
            out_shape.append(jax.ShapeDtypeStruct((m, GROUP_W), dtype))
        else:
            per_seq = seq_len // tm
            out_specs.append(pl.BlockSpec((1, GROUP_W, tm), lambda i: (i // per_seq, 0, i % per_seq)))
            out_shape.append(jax.ShapeDtypeStruct((m // seq_len, GROUP_W, seq_len), dtype))
    return pl.pallas_call(
        functools.partial(_rms_proj_body, plan=tuple((c, kind) for c, kind, _ in plan)),
        grid=(m // tm,),
        in_specs=[pl.BlockSpec((tm, d), lambda i: (i, 0)),
                  pl.BlockSpec((1, d), lambda i: (0, 0)),
                  pl.BlockSpec((d, n), lambda i: (0, 0), pipeline_mode=pl.Buffered(1))],
        out_specs=out_specs,
        out_shape=out_shape,
        compiler_params=_params("parallel"),
        name="rms_proj",
    )(x, g.reshape(1, d), w)


def _conv_body(a_ref, b_ref, g_ref, hist_ref, wdw_ref, bdw_ref, lng_ref, lnb_ref, wpw_ref, bpw_ref,
               y_ref, new_ref, shift_ref, s_ref, taps_ref, *, t, rows, nb):
    ext_ref = shift_ref.at[0]
    ti = pl.program_id(1)
    lead = 32 - CONV_HIST
    n_ext = shift_ref.shape[1]
    tile_rows = min(rows, SUBLANES)
    split = (rows // tile_rows, tile_rows, GROUP_W)

    for bi in range(nb):
        seq = slice(bi * t, (bi + 1) * t)

        @pl.when(ti == 0)
        def _():
            ext_ref[...] = jnp.zeros((n_ext, GROUP_W), F32)
            ext_ref[lead:32, :] = hist_ref[bi]

        @pl.when(ti > 0)
        def _():
            ext_ref[lead:32, :] = ext_ref[t + lead:t + 32, :]

        ext_ref[32:32 + t, :] = a_ref[seq, :] * _sigmoid(b_ref[seq, :])
        ext = ext_ref[...]
        for r in range(1, SUBLANES):
            shift_ref[r] = pltpu.roll(ext, n_ext - r, axis=0)
        for j in range(CONV_K):
            taps_ref[j] = jnp.broadcast_to(wdw_ref[j:j + 1, :], (SUBLANES, GROUP_W))

        new_ref[bi] = ext_ref[t + lead:t + 32, :]

        for c in range(t // rows):
            base = c * rows
            acc = jnp.broadcast_to(bdw_ref[...], (rows, GROUP_W)).reshape(split)
            for j in range(CONV_K):
                blocks, r = divmod(lead + j, SUBLANES)
                start = base + blocks * SUBLANES
                acc = acc + (taps_ref[j, 0:tile_rows, :][None]
                             * shift_ref[r, start:start + rows, :].reshape(split))
            acc = acc.reshape(rows, GROUP_W)
            mu = jnp.mean(acc, axis=-1, keepdims=True)
            xc = acc - mu
            var = jnp.mean(xc * xc, axis=-1, keepdims=True)
            yn = xc * lax.rsqrt(var + EPS) * lng_ref[...] + lnb_ref[...]
            s_ref[bi * t + base:bi * t + base + rows, :] = _silu(yn).astype(BF16)

    y = jnp.dot(s_ref[...], wpw_ref[...], preferred_element_type=F32) + bpw_ref[...]
    y_ref[...] = (y * _silu(g_ref[...])).astype(y_ref.dtype)


def conv_mixer(a, b, g, hist, w_dw, b_dw, ln_g, ln_b, w_pw, b_pw, n, l, t, nb, out_dtype):
    w = a.shape[1]
    rows = min(t, 32)
    tiles = l // t
    tile = pl.BlockSpec((nb * t, w), lambda i, j: (i * tiles + j, 0))
    state = pl.BlockSpec((nb, CONV_HIST, w), lambda i, j: (i, 0, 0))
    vec = pl.BlockSpec((1, w), lambda i, j: (0, 0))
    return pl.pallas_call(
        functools.partial(_conv_body, t=t, rows=rows, nb=nb),
        grid=(n // nb, tiles),
        in_specs=[tile, tile, tile, state,
                  pl.BlockSpec((CONV_K, w), lambda i, j: (0, 0)),
                  vec, vec, vec,
                  pl.BlockSpec((w, w), lambda i, j: (0, 0)),
                  vec],
        out_specs=[tile, state],
        out_shape=[jax.ShapeDtypeStruct((n * l, w), out_dtype),
                   jax.ShapeDtypeStruct((n, CONV_HIST, w), F32)],
        scratch_shapes=[pltpu.VMEM((SUBLANES, pl.cdiv(32 + t, SUBLANES) * SUBLANES, w), F32),
                        pltpu.VMEM((nb * t, w), BF16), pltpu.VMEM((CONV_K, SUBLANES, w), F32)],
        compiler_params=_params("parallel", "arbitrary"),
        name="conv_mixer",
    )(a, b, g, hist, w_dw, b_dw.reshape(1, w), ln_g.reshape(1, w), ln_b.reshape(1, w),
      w_pw, b_pw.reshape(1, w))


def _pool_body(u_ref, g_ref, hist_ref, w_ref, scale_ref, y_ref, new_ref, ext_ref, d_ref,
               *, t, pos0, nb):
    ti = pl.program_id(1)
    lead = 16 - POOL_HIST
    pos = pos0 + ti * t + lax.broadcasted_iota(jnp.int32, (t, POOL_GROUP), 0)

    for bi in range(nb):
        seq = slice(bi * t, (bi + 1) * t)

        @pl.when(ti == 0)
        def _():
            ext_ref[lead:16, :] = hist_ref[bi]

        @pl.when(ti > 0)
        def _():
            ext_ref[lead:16, :] = ext_ref[t + lead:t + 16, :]

        ext_ref[16:16 + t, :] = u_ref[seq, :]
        new_ref[bi] = ext_ref[t + lead:t + 16, :]
        for gi, win in enumerate(POOL_WINDOWS):
            sl = slice(gi * POOL_GROUP, (gi + 1) * POOL_GROUP)
            u = u_ref[seq, sl]
            tot = u
            for k in range(1, win):
                tot = tot + ext_ref[16 - k:16 - k + t, sl]
            cnt = jnp.minimum(win, pos + 1).astype(F32)
            d_ref[seq, sl] = (tot / cnt - u).astype(BF16)

    for gi in range(len(POOL_WINDOWS)):
        sl = slice(gi * POOL_GROUP, (gi + 1) * POOL_GROUP)
        y = jnp.dot(d_ref[:, sl], w_ref[gi], preferred_element_type=F32) * scale_ref[:, sl]
        y_ref[:, sl] = (y * _silu(g_ref[:, sl])).astype(y_ref.dtype)


def pool_mixer(u, g, hist, w_pool, scale, n, l, t, nb, pos0, out_dtype):
    w = u.shape[1]
    tiles = l // t
    tile = pl.BlockSpec((nb * t, w), lambda i, j: (i * tiles + j, 0))
    state = pl.BlockSpec((nb, POOL_HIST, w), lambda i, j: (i, 0, 0))
    return pl.pallas_call(
        functools.partial(_pool_body, t=t, pos0=pos0, nb=nb),
        grid=(n // nb, tiles),
        in_specs=[tile, tile, state,
                  pl.BlockSpec(w_pool.shape, lambda i, j: (0, 0, 0)),
                  pl.BlockSpec((1, w), lambda i, j: (0, 0))],
        out_specs=[tile, state],
        out_shape=[jax.ShapeDtypeStruct((n * l, w), out_dtype),
                   jax.ShapeDtypeStruct((n, POOL_HIST, w), F32)],
        scratch_shapes=[pltpu.VMEM((16 + t, w), F32), pltpu.VMEM((nb * t, w), BF16)],
        compiler_params=_params("parallel", "arbitrary"),
        name="pool_mixer",
    )(u, g, hist, w_pool, scale.reshape(1, w))


def _suffix_sum_matrix(blk):
    r = lax.broadcasted_iota(jnp.int32, (blk, blk), 0)
    c = lax.broadcasted_iota(jnp.int32, (blk, blk), 1)
    half = jnp.concatenate([(r >= c).astype(BF16), jnp.ones((blk, blk), BF16)], axis=1)
    return jnp.concatenate([half, half], axis=0)


def _sb_block(nz, tri, carry, mask):
    r = jnp.dot(_sb_log_keep(nz, mask), tri, preferred_element_type=F32)
    return _sb_weights(r, carry, nz, mask)


def _sb_log_keep(nz, mask):
    lg = jnp.minimum(nz, 0.0) - jnp.log(1.0 + jnp.exp2(jnp.abs(nz) * (-LOG2_E)))
    if mask is not None:
        lg = jnp.where(mask, lg, 0.0)
    hi = lg.astype(BF16)
    lo = (lg - hi.astype(F32)).astype(BF16)
    return jnp.concatenate([hi, lo], axis=1)


def _sb_weights(r, carry, nz, mask):
    blk = nz.shape[1]
    a = jnp.exp(r[:, :blk] + carry - nz)
    if mask is not None:
        a = jnp.where(mask, a, 0.0)
    return a, carry + r[:, blk:]


def _sb_prompt_body(bias_ref, q_ref, kt_ref, v_ref, g_ref, tri_ref, o_ref,
                    qq_ref, nz_ref, hl_ref, r_ref, aa_ref, carry_ref, acc_ref):
    tq = SB_PROMPT_BLOCK
    sub = LANES
    halves = tq // sub
    pairs = SB_HEADS // 2
    i = pl.program_id(1)
    even_q = lax.broadcasted_iota(jnp.int32, (tq, LANES), 1) < SB_HEAD_DIM
    even_v = lax.broadcasted_iota(jnp.int32, (sub, LANES), 1) < SB_HEAD_DIM
    row = lax.broadcasted_iota(jnp.int32, (2 * tq, sub), 0)
    top = row < tq
    col = lax.broadcasted_iota(jnp.int32, (2 * tq, sub), 1)
    qpos = jnp.where(top, row, row - tq)
    zero = jnp.zeros((), BF16)

    for p in range(pairs):
        qp = q_ref[:, p * LANES:(p + 1) * LANES] * (-(SB_HEAD_DIM ** -0.5))
        qq_ref[p] = jnp.concatenate([jnp.where(even_q, qp, 0.0), jnp.where(even_q, 0.0, qp)],
                                    axis=0).astype(BF16)
    carry_ref[...] = jnp.zeros_like(carry_ref)
    acc_ref[...] = jnp.zeros_like(acc_ref)

    def logits(j, slot, p):
        start = pl.multiple_of(j * tq, tq)
        kt = kt_ref[0, p * LANES:(p + 1) * LANES, pl.ds(start, tq)].astype(BF16)
        nbias = jnp.where(top, -bias_ref[2 * p], -bias_ref[2 * p + 1])
        nz = jnp.dot(qq_ref[p], kt, preferred_element_type=F32)
        for h in range(halves):
            nz_ref[slot, p, :, h * sub:(h + 1) * sub] = nz[:, h * sub:(h + 1) * sub] + nbias

    def accumulate(j, slot, p):
        start = pl.multiple_of(j * tq, tq)
        vb = v_ref[pl.ds(start, tq), p * LANES:(p + 1) * LANES]
        vv = []
        for h in range(halves):
            vh = vb[h * sub:(h + 1) * sub]
            vv += [jnp.where(even_v, vh, zero), jnp.where(even_v, zero, vh)]
        acc_ref[p] += jnp.dot(aa_ref[slot, p], jnp.concatenate(vv, axis=0),
                              preferred_element_type=F32)

    def key_block(slot, diagonal, next_j=None, prev_j=None):
        masks = [col + h * sub < qpos if diagonal else None for h in range(halves)]
        for p in range(pairs):
            for h in range(halves):
                hl_ref[p, h] = _sb_log_keep(nz_ref[slot, p, :, h * sub:(h + 1) * sub], masks[h])
            if next_j is not None:
                logits(next_j, 1 - slot, p)
        for p in range(pairs):
            for h in range(halves):
                r_ref[p, h] = jnp.dot(hl_ref[p, h], tri_ref[...], preferred_element_type=F32)
        for p in range(pairs):
            carry = carry_ref[p]
            for h in reversed(range(halves)):
                a, carry = _sb_weights(r_ref[p, h], carry,
                                       nz_ref[slot, p, :, h * sub:(h + 1) * sub], masks[h])
                ab = a.astype(BF16)
                aa_ref[slot, p, :, 2 * h * sub:(2 * h + 1) * sub] = ab[:tq]
                aa_ref[slot, p, :, (2 * h + 1) * sub:(2 * h + 2) * sub] = ab[tq:]
            carry_ref[p] = carry
            if prev_j is not None:
                accumulate(prev_j, 1 - slot, p)

    for p in range(pairs):
        logits(i, 0, p)
    key_block(0, True, next_j=jnp.maximum(i - 1, 0))

    def two_blocks(m, _):
        j = i - 1 - 2 * m
        key_block(1, False, next_j=j - 1, prev_j=j + 1)
        key_block(0, False, next_j=jnp.maximum(j - 2, 0), prev_j=j)
        return 0

    lax.fori_loop(0, i // 2, two_blocks, 0)

    @pl.when(i % 2 == 1)
    def _():
        key_block(1, False, prev_j=1)
        for p in range(pairs):
            accumulate(0, 1, p)

    @pl.when(i % 2 == 0)
    def _():
        for p in range(pairs):
            accumulate(0, 0, p)

    for p in range(pairs):
        sl = slice(p * LANES, (p + 1) * LANES)
        o_ref[:, sl] = (acc_ref[p] * _silu(g_ref[:, sl])).astype(o_ref.dtype)


def sb_prompt(q, kt, v, g, bias):
    n, w, l = kt.shape
    tq = SB_PROMPT_BLOCK
    sub = LANES
    halves = tq // sub
    pairs = SB_HEADS // 2
    tiles = l // tq
    tile = pl.BlockSpec((tq, w), lambda b, i: (b * tiles + i, 0))
    return pl.pallas_call(
        _sb_prompt_body,
        grid=(n, tiles),
        in_specs=[pl.BlockSpec(memory_space=pltpu.SMEM), tile,
                  pl.BlockSpec((1, w, l), lambda b, i: (b, 0, 0)),
                  pl.BlockSpec((l, w), lambda b, i: (b, 0)), tile,
                  pl.BlockSpec((2 * sub, 2 * sub), lambda b, i: (0, 0))],
        out_specs=tile,
        out_shape=jax.ShapeDtypeStruct((n * l, w), BF16),
        scratch_shapes=[pltpu.VMEM((pairs, 2 * tq, LANES), BF16),
                        pltpu.VMEM((2, pairs, 2 * tq, tq), F32),
                        pltpu.VMEM((pairs, halves, 2 * tq, 2 * sub), BF16),
                        pltpu.VMEM((pairs, halves, 2 * tq, 2 * sub), F32),
                        pltpu.VMEM((2, pairs, tq, halves * 2 * sub), BF16),
                        pltpu.VMEM((pairs, 2 * tq, sub), F32),
                        pltpu.VMEM((pairs, tq, LANES), F32)],
        compiler_params=_params("parallel", "arbitrary"),
        name="sb_prompt",
    )(bias, q, kt, v, g, _suffix_sum_matrix(sub))


def _sb_sample_body(pt_ref, q_ref, kn_ref, vn_ref, g_ref, nbias_ref, tri_ref, ckt_hbm, cvt_hbm, o_ref,
                    kbuf_ref, vbuf_ref, sem_ref, qall_ref, carry_ref, acc_ref, pad_ref, ktb_ref,
                    vtb_ref, nz_ref, hl_ref, r_ref, a_ref, *, ls, pages, n_pages, layer):
    blk = PAGE_SIZE
    rows = ls * SB_HEADS
    w = SB_HEADS * SB_HEAD_DIM
    b = pl.program_id(0)
    s = pl.program_id(1)
    steps = pl.num_programs(1)
    total = pl.num_programs(0) * steps
    step = b * steps + s
    nt = (((1,), (1,)), ((), ()))
    head_mask = (lax.broadcasted_iota(jnp.int32, (SB_HEADS, w), 1) // SB_HEAD_DIM
                 == lax.broadcasted_iota(jnp.int32, (SB_HEADS, w), 0))
    tri = tri_ref[...]

    def page_copies(at):
        seq = at // steps
        group = at % steps
        slot = at % SB_SAMPLE_RING
        copies = []
        for r in range(pages):
            page = pt_ref[seq * n_pages + n_pages - 1 - (group * pages + r)]
            copies.append(pltpu.make_async_copy(ckt_hbm.at[layer, page], kbuf_ref.at[slot, r],
                                                sem_ref.at[slot, r]))
            copies.append(pltpu.make_async_copy(cvt_hbm.at[layer, page], vbuf_ref.at[slot, r],
                                                sem_ref.at[slot, pages + r]))
        return copies

    @pl.when(step == 0)
    def _():
        for ahead in range(SB_SAMPLE_RING - 1):
            for copy in page_copies(ahead):
                copy.start()

    @pl.when(step + (SB_SAMPLE_RING - 1) < total)
    def _():
        for copy in page_copies(step + (SB_SAMPLE_RING - 1)):
            copy.start()

    @pl.when(s == 0)
    def _():
        for t in range(ls):
            qt = q_ref[pl.ds(b * ls + t, 1), :] * (-(SB_HEAD_DIM ** -0.5))
            qall_ref[t * SB_HEADS:(t + 1) * SB_HEADS, :] = jnp.where(head_mask, qt, 0.0).astype(BF16)
        key = lax.broadcasted_iota(jnp.int32, (rows, blk), 1)
        qry = lax.broadcasted_iota(jnp.int32, (rows, blk), 0) // SB_HEADS
        pad_ref[...] = jnp.zeros_like(pad_ref)
        for t in range(ls):
            pad_ref[t:t + 1, :] = kn_ref[pl.ds(b * ls + t, 1), :]
        nz = lax.dot_general(qall_ref[...], pad_ref[...].astype(BF16), nt, preferred_element_type=F32)
        for t in range(ls):
            pad_ref[t:t + 1, :] = vn_ref[pl.ds(b * ls + t, 1), :]
        a, carry = _sb_block(nz + nbias_ref[...], tri, jnp.zeros((rows, blk), F32), key < qry)
        carry_ref[...] = carry
        acc_ref[...] = jnp.dot(a.astype(BF16), pad_ref[...].astype(BF16), preferred_element_type=F32)

    for copy in page_copies(step):
        copy.wait()
    slot = step % SB_SAMPLE_RING

    for r in range(pages):
        ktb_ref[:, r * blk:(r + 1) * blk] = kbuf_ref[slot, r].astype(BF16)
        vtb_ref[:, r * blk:(r + 1) * blk] = vbuf_ref[slot, r].astype(BF16)
    nz_ref[...] = (jnp.dot(qall_ref[...], ktb_ref[...], preferred_element_type=F32)
                   + jnp.concatenate([nbias_ref[...]] * pages, axis=1))
    for r in range(pages):
        hl_ref[r * rows:(r + 1) * rows, :] = _sb_log_keep(nz_ref[:, r * blk:(r + 1) * blk], None)
    r_ref[...] = jnp.dot(hl_ref[...], tri, preferred_element_type=F32)
    carry = carry_ref[...]
    for r in range(pages):
        a, carry = _sb_weights(r_ref[r * rows:(r + 1) * rows, :], carry,
                               nz_ref[:, r * blk:(r + 1) * blk], None)
        a_ref[:, r * blk:(r + 1) * blk] = a.astype(BF16)
    carry_ref[...] = carry
    acc_ref[...] += lax.dot_general(a_ref[...], vtb_ref[...], nt, preferred_element_type=F32)

    @pl.when(s == steps - 1)
    def _():
        for t in range(ls):
            o = jnp.where(head_mask, acc_ref[t * SB_HEADS:(t + 1) * SB_HEADS, :], 0.0)
            o = jnp.sum(o, axis=0, keepdims=True)
            gate = g_ref[pl.ds(b * ls + t, 1), :]
            o_ref[pl.ds(b * ls + t, 1), :] = (o * _silu(gate)).astype(o_ref.dtype)


def sb_sample(q, k_new, v_new, g, bias, cache_kt, cache_vt, page_table, layer, ls):
    m, w = q.shape
    n, n_pages = page_table.shape
    pages = SB_SAMPLE_PAGES
    rows = ls * SB_HEADS
    blk = PAGE_SIZE
    whole = pl.BlockSpec((m, w), lambda b, s, pt: (0, 0))
    nbias_rows = jnp.broadcast_to(-jnp.tile(bias, ls)[:, None], (rows, blk)).astype(F32)
    grid_spec = pltpu.PrefetchScalarGridSpec(
        num_scalar_prefetch=1,
        grid=(n, n_pages // pages),
        in_specs=[whole, whole, whole, whole,
                  pl.BlockSpec((rows, blk), lambda b, s, pt: (0, 0)),
                  pl.BlockSpec((2 * blk, 2 * blk), lambda b, s, pt: (0, 0)),
                  pl.BlockSpec(memory_space=pl.ANY), pl.BlockSpec(memory_space=pl.ANY)],
        out_specs=whole,
        scratch_shapes=[pltpu.VMEM((SB_SAMPLE_RING, pages, w, blk), F32),
                        pltpu.VMEM((SB_SAMPLE_RING, pages, w, blk), F32),
                        pltpu.SemaphoreType.DMA((SB_SAMPLE_RING, 2 * pages)),
                        pltpu.VMEM((rows, w), BF16), pltpu.VMEM((rows, blk), F32),
                        pltpu.VMEM((rows, w), F32), pltpu.VMEM((blk, w), F32),
                        pltpu.VMEM((w, pages * blk), BF16), pltpu.VMEM((w, pages * blk), BF16),
                        pltpu.VMEM((rows, pages * blk), F32),
                        pltpu.VMEM((pages * rows, 2 * blk), BF16),
                        pltpu.VMEM((pages * rows, 2 * blk), F32),
                        pltpu.VMEM((rows, pages * blk), BF16)],
    )
    return pl.pallas_call(
        functools.partial(_sb_sample_body, ls=ls, pages=pages, n_pages=n_pages, layer=layer),
        grid_spec=grid_spec,
        out_shape=jax.ShapeDtypeStruct((m, w), F32),
        compiler_params=_params("arbitrary", "arbitrary"),
        name="sb_sample",
    )(page_table.reshape(-1), q, k_new, v_new, g, nbias_rows, _suffix_sum_matrix(blk),
      cache_kt, cache_vt)


def _softmax_rows(s):
    e = jnp.exp(s - jnp.max(s, axis=-1, keepdims=True))
    return e / jnp.sum(e, axis=-1, keepdims=True)


def _mem_body(q_ref, g_ref, mk_ref, mv_ref, o_ref, *, t, nb):
    nt = (((1,), (1,)), ((), ()))
    for bi in range(nb):
        seq = slice(bi * t, (bi + 1) * t)
        for h in range(MEM_HEADS):
            sl = slice(h * MEM_HEAD_DIM, (h + 1) * MEM_HEAD_DIM)
            s = lax.dot_general(q_ref[seq, sl].astype(BF16), mk_ref[bi, :, sl].astype(BF16), nt,
                                preferred_element_type=F32) * (MEM_HEAD_DIM ** -0.5)
            o = jnp.dot(_softmax_rows(s).astype(BF16), mv_ref[bi, :, sl].astype(BF16),
                        preferred_element_type=F32)
            o_ref[seq, sl] = (o * _silu(g_ref[seq, sl])).astype(o_ref.dtype)


def _mem_paired_body(q_ref, g_ref, mk_ref, mv_ref, o_ref, *, t, nb):
    nt = (((1,), (1,)), ((), ()))
    n_rows = mk_ref.shape[2] * mk_ref.shape[3]
    row_head = lax.broadcasted_iota(jnp.int32, (MEM_HEADS * t, n_rows), 1) % MEM_HEADS
    q_head = lax.broadcasted_iota(jnp.int32, (MEM_HEADS * t, n_rows), 0) // t
    for bi in range(nb):
        seq = slice(bi * t, (bi + 1) * t)
        heads = [slice(h * MEM_HEAD_DIM, (h + 1) * MEM_HEAD_DIM) for h in range(MEM_HEADS)]
        q = jnp.concatenate([q_ref[seq, sl] for sl in heads], axis=0).astype(BF16)
        mk = mk_ref[0, bi].reshape(n_rows, MEM_HEAD_DIM).astype(BF16)
        mv = mv_ref[0, bi].reshape(n_rows, MEM_HEAD_DIM).astype(BF16)
        s = lax.dot_general(q, mk, nt, preferred_element_type=F32) * (MEM_HEAD_DIM ** -0.5)
        p = _softmax_rows(jnp.where(row_head == q_head, s, -jnp.inf))
        o = jnp.dot(p.astype(BF16), mv, preferred_element_type=F32)
        for h, sl in enumerate(heads):
            o_ref[seq, sl] = (o[h * t:(h + 1) * t] * _silu(g_ref[seq, sl])).astype(o_ref.dtype)


def mem_attend(q, g, mk, mv, n, l, t, nb, out_dtype, layer=None):
    w = q.shape[1]
    tiles = l // t
    tile = pl.BlockSpec((nb * t, w), lambda b, i: (b * tiles + i, 0))
    if layer is None:
        mem = pl.BlockSpec((nb,) + mk.shape[1:], lambda b, i: (b, 0, 0))
    else:
        mem = pl.BlockSpec((1, nb) + mk.shape[2:], lambda b, i: (layer, b, 0, 0, 0))
    return pl.pallas_call(
        functools.partial(_mem_body if layer is None else _mem_paired_body, t=t, nb=nb),
        grid=(n // nb, tiles),
        in_specs=[tile, tile, mem, mem],
        out_specs=tile,
        out_shape=jax.ShapeDtypeStruct((n * l, w), out_dtype),
        compiler_params=_params("parallel", "arbitrary"),
        name="mem_attend",
    )(q, g, mk, mv)


def _out_proj_body(c_ref, p_ref, s_ref, m_ref, w_ref, pg_ref, x_ref, fg_ref, o_ref, *, final):
    acc = None
    for gi, part in enumerate((c_ref, p_ref, s_ref, m_ref)):
        d = jnp.dot(part[...].astype(BF16), w_ref[gi * GROUP_W:(gi + 1) * GROUP_W, :],
                    preferred_element_type=F32)
        acc = d if acc is None else acc + d
    y = x_ref[...] + _rms(acc, pg_ref[...])
    if final:
        y = _rms(y, fg_ref[...])
    o_ref[...] = y


def out_proj(parts, w, post_g, x, final_g, tm, final):
    m, d = x.shape
    part = pl.BlockSpec((tm, GROUP_W), lambda i: (i, 0))
    vec = pl.BlockSpec((1, d), lambda i: (0, 0))
    row = pl.BlockSpec((tm, d), lambda i: (i, 0))
    return pl.pallas_call(
        functools.partial(_out_proj_body, final=final),
        grid=(m // tm,),
        in_specs=[part, part, part, part, pl.BlockSpec(w.shape, lambda i: (0, 0)), vec, row, vec],
        out_specs=row,
        out_shape=jax.ShapeDtypeStruct((m, d), F32),
        compiler_params=_params("parallel"),
        name="out_proj",
    )(*parts, w, post_g.reshape(1, d), x, final_g.reshape(1, d))


def _mixer_layer(x, n, l, conv_hist, pool_hist, pos0, sb_fn, mem_fn, lw, final_g, final, tiles,
                 keys_on_lanes):
    (pre_g, post_g, w_in, w_dw, b_dw, ln_g, ln_b, w_pw, b_pw, pool_w, pool_scale, sb_bias, w_out) = lw
    tm, tm_out, t_local, nb, mix_dtype = tiles
    plan = [(c, "rows", F32) for c in range(len(MIX_GROUPS))]
    if keys_on_lanes:
        plan[K_GROUP] = (K_GROUP, "cols", F32)
        plan[V_GROUP:V_GROUP + 1] = [(V_GROUP, "cols", F32), (V_GROUP, "rows", BF16)]
    z = list(rms_proj(x, pre_g, w_in, tm, tuple(plan), l))
    v_rows = z.pop(V_GROUP + 1) if keys_on_lanes else z[V_GROUP]
    ca, cb, cg, pu, pg, q, k, v, sg, mq, mg = z
    y_c, conv_new = conv_mixer(ca, cb, cg, conv_hist, w_dw, b_dw, ln_g, ln_b, w_pw, b_pw, n, l,
                               t_local, nb, mix_dtype)
    y_p, pool_new = pool_mixer(pu, pg, pool_hist, pool_w, pool_scale, n, l, t_local, nb, pos0,
                               mix_dtype)
    y_s = sb_fn(q, k, v_rows, sg, sb_bias)
    y_m = mem_fn(mq, mg, t_local, nb, mix_dtype)
    x = out_proj([y_c, y_p, y_s, y_m], w_out, post_g, x, final_g, tm_out, final)
    return x, conv_new, pool_new, k, v


def kernel(x_prompt, x_sample, mem_prompt, cache_k, cache_v, cache_mem_k, cache_mem_v, state_conv,
           state_pool, page_table, pre_g, post_g, w_in, conv_w_dw, conv_b_dw, conv_ln_g, conv_ln_b,
           conv_w_pw, conv_b_pw, pool_w, pool_scale, sb_bias, mem_g, w_mem_kv, w_out, final_g):
    bp, lp, d = x_prompt.shape
    bs, ls, _ = x_sample.shape
    depth = w_in.shape[0]
    n_mem = mem_prompt.shape[1]
    n_phys = cache_k.shape[1]
    past = page_table.shape[1] * PAGE_SIZE
    w = GROUP_W

    xp = x_prompt.reshape(bp * lp, d)
    xs = x_sample.reshape(bs * ls, d)
    mem = mem_prompt.reshape(bp * n_mem, d)
    ckt = jnp.transpose(cache_k, (0, 1, 3, 4, 2)).reshape(depth, n_phys, w, PAGE_SIZE)
    cvt = jnp.transpose(cache_v, (0, 1, 3, 4, 2)).reshape(depth, n_phys, w, PAGE_SIZE)
    pair_shape = (depth, bs, n_mem // 2, 2 * MEM_HEADS, MEM_HEAD_DIM)
    cmk = cache_mem_k.reshape(pair_shape)
    cmv = cache_mem_v.reshape(pair_shape)
    zero_conv = jnp.zeros((bp, CONV_HIST, w), F32)
    zero_pool = jnp.zeros((bp, POOL_HIST, w), F32)
    prompt_tiles = (512, 512, 512, 1, BF16)
    sample_tiles = (bs * ls, bs * ls, ls, 8, F32)

    outs = [[] for _ in range(10)]
    for l in range(depth):
        final = l == depth - 1
        lw = (pre_g[l], post_g[l], w_in[l].astype(BF16), conv_w_dw[l], conv_b_dw[l], conv_ln_g[l],
              conv_ln_b[l], conv_w_pw[l].astype(BF16), conv_b_pw[l], pool_w[l].astype(BF16),
              pool_scale[l], sb_bias[l], w_out[l].astype(BF16))
        mk_p, mv_p = [a.reshape(bp, n_mem, w)
                      for a in rms_proj(mem, mem_g[l], w_mem_kv[l].astype(BF16), 256)]

        def mem_prompt_fn(q, g, t, nb, dtype, mk_p=mk_p, mv_p=mv_p):
            return mem_attend(q, g, mk_p, mv_p, bp, lp, t, nb, dtype)

        xp, c_p, p_p, k_p, v_p = _mixer_layer(
            xp, bp, lp, zero_conv, zero_pool, 0, sb_prompt, mem_prompt_fn, lw, final_g, final,
            prompt_tiles, True)

        def sb_sample_fn(q, k, v, g, bias, l=l):
            return sb_sample(q, k, v, g, bias, ckt, cvt, page_table, l, ls)

        def mem_sample_fn(q, g, t, nb, dtype, l=l):
            return mem_attend(q, g, cmk, cmv, bs, ls, t, nb, dtype, layer=l)

        xs, c_s, p_s, k_s, v_s = _mixer_layer(
            xs, bs, ls, state_conv[l], state_pool[l], past, sb_sample_fn, mem_sample_fn, lw,
            final_g, final, sample_tiles, False)
        for lst, a in zip(outs, (k_p, v_p, k_s, v_s, c_p, c_s, p_p, p_s, mk_p, mv_p)):
            lst.append(a)

    k_p, v_p, k_s, v_s, c_p, c_s, p_p, p_s, mk_p, mv_p = [jnp.stack(a) for a in outs]
    heads_first = (depth, bp, SB_HEADS, SB_HEAD_DIM, lp)
    return (xp.reshape(bp, lp, d), xs.reshape(bs, ls, d),
            jnp.transpose(k_p.reshape(heads_first), (0, 1, 4, 2, 3)),
            jnp.transpose(v_p.reshape(heads_first), (0, 1, 4, 2, 3)),
            k_s.reshape(depth, bs, ls, SB_HEADS, SB_HEAD_DIM),
            v_s.reshape(depth, bs, ls, SB_HEADS, SB_HEAD_DIM),
            c_p, c_s, p_p, p_s,
            mk_p.reshape(depth, bp, n_mem, MEM_HEADS, MEM_HEAD_DIM),
            mv_p.reshape(depth, bp, n_mem, MEM_HEADS, MEM_HEAD_DIM))
```

```python
import functools

import jax
import jax.numpy as jnp
from jax import lax
from jax.experimental import pallas as pl
from jax.experimental.pallas import tpu as pltpu

F32 = jnp.float32
BF16 = jnp.bfloat16

EPS = 1e-6
GROUP_W = 512
MIX_GROUPS = ("conv_a", "conv_glu_gate", "conv_silu_gate", "pool_u", "pool_gate", "sb_q", "sb_k",
              "sb_v", "sb_gate", "mem_q", "mem_gate")
K_GROUP = MIX_GROUPS.index("sb_k")
V_GROUP = MIX_GROUPS.index("sb_v")
CONV_K = 31
CONV_HIST = CONV_K - 1
POOL_WINDOWS = (2, 4, 8, 16)
POOL_GROUP = 128
POOL_HIST = max(POOL_WINDOWS) - 1
SB_HEADS = 8
SB_HEAD_DIM = 64
MEM_HEADS = 4
MEM_HEAD_DIM = 128
PAGE_SIZE = 128
LANES = 128
SUBLANES = 8
SB_PROMPT_BLOCK = 256
SB_SAMPLE_PAGES = 8
SB_SAMPLE_RING = 3
LOG2_E = 1.4426950408889634
VMEM_LIMIT_BYTES = 56 * 1024 * 1024


def _params(*semantics):
    return pltpu.CompilerParams(dimension_semantics=semantics, vmem_limit_bytes=VMEM_LIMIT_BYTES)


def _sigmoid(x):
    return 1.0 / (1.0 + jnp.exp(-x))


def _silu(x):
    return x * _sigmoid(x)


def _rms(x, g):
    return x * lax.rsqrt(jnp.mean(x * x, axis=-1, keepdims=True) + EPS) * g


def _rms_proj_body(x_ref, g_ref, w_ref, *o_refs, plan):
    h = _rms(x_ref[...], g_ref[...]).astype(BF16)
    done = {}
    for (c, kind), o_ref in zip(plan, o_refs):
        if c not in done:
            done = {c: jnp.dot(h, w_ref[:, c * GROUP_W:(c + 1) * GROUP_W],
                               preferred_element_type=F32)}
        y = done[c]
        if kind == "rows":
            o_ref[...] = y.astype(o_ref.dtype)
        else:
            o_ref[0] = y.T


def rms_proj(x, g, w, tm, plan=None, seq_len=None):
    m, d = x.shape
    n = w.shape[1]
    if plan is None:
        plan = tuple((c, "rows", F32) for c in range(n // GROUP_W))
    out_specs, out_shape = [], []
    for _, kind, dtype in plan:
        if kind == "rows":
            out_specs.append(pl.BlockSpec((tm, GROUP_W), lambda i: (i, 0)))
            out_shape.append(jax.ShapeDtypeStruct((m, GROUP_W), dtype))
        else:
            per_seq = seq_len // tm
            out_specs.append(pl.BlockSpec((1, GROUP_W, tm), lambda i: (i // per_seq, 0, i % per_seq)))
            out_shape.append(jax.ShapeDtypeStruct((m // seq_len, GROUP_W, seq_len), dtype))
    return pl.pallas_call(
        functools.partial(_rms_proj_body, plan=tuple((c, kind) for c, kind, _ in plan)),
        grid=(m // tm,),
        in_specs=[pl.BlockSpec((tm, d), lambda i: (i, 0)),
                  pl.BlockSpec((1, d), lambda i: (0, 0)),
                  pl.BlockSpec((d, n), lambda i: (0, 0), pipeline_mode=pl.Buffered(1))],
        out_specs=out_specs,
        out_shape=out_shape,
        compiler_params=_params("parallel"),
        name="rms_proj",
    )(x, g.reshape(1, d), w)


def _conv_body(a_ref, b_ref, g_ref, hist_ref, wdw_ref, bdw_ref, lng_ref, lnb_ref, wpw_ref, bpw_ref,
               y_ref, new_ref, shift_ref, s_ref, taps_ref, *, t, rows, nb):
    ext_ref = shift_ref.at[0]
    ti = pl.program_id(1)
    lead = 32 - CONV_HIST
    n_ext = shift_ref.shape[1]
    tile_rows = min(rows, SUBLANES)
    split = (rows // tile_rows, tile_rows, GROUP_W)

    for bi in range(nb):
        seq = slice(bi * t, (bi + 1) * t)

        @pl.when(ti == 0)
        def _():
            ext_ref[...] = jnp.zeros((n_ext, GROUP_W), F32)
            ext_ref[lead:32, :] = hist_ref[bi]

        @pl.when(ti > 0)
        def _():
            ext_ref[lead:32, :] = ext_ref[t + lead:t + 32, :]

        ext_ref[32:32 + t, :] = a_ref[seq, :] * _sigmoid(b_ref[seq, :])
        ext = ext_ref[...]
        for r in range(1, SUBLANES):
            shift_ref[r] = pltpu.roll(ext, n_ext - r, axis=0)
        for j in range(CONV_K):
            taps_ref[j] = jnp.broadcast_to(wdw_ref[j:j + 1, :], (SUBLANES, GROUP_W))

        new_ref[bi] = ext_ref[t + lead:t + 32, :]

        for c in range(t // rows):
            base = c * rows
            acc = jnp.broadcast_to(bdw_ref[...], (rows, GROUP_W)).reshape(split)
            for j in range(CONV_K):
                blocks, r = divmod(lead + j, SUBLANES)
                start = base + blocks * SUBLANES
                acc = acc + (taps_ref[j, 0:tile_rows, :][None]
                             * shift_ref[r, start:start + rows, :].reshape(split))
            acc = acc.reshape(rows, GROUP_W)
            mu = jnp.mean(acc, axis=-1, keepdims=True)
            xc = acc - mu
            var = jnp.mean(xc * xc, axis=-1, keepdims=True)
            yn = xc * lax.rsqrt(var + EPS) * lng_ref[...] + lnb_ref[...]
            s_ref[bi * t + base:bi * t + base + rows, :] = _silu(yn).astype(BF16)

    y = jnp.dot(s_ref[...], wpw_ref[...], preferred_element_type=F32) + bpw_ref[...]
    y_ref[...] = (y * _silu(g_ref[...])).astype(y_ref.dtype)


def conv_mixer(a, b, g, hist, w_dw, b_dw, ln_g, ln_b, w_pw, b_pw, n, l, t, nb, out_dtype):
    w = a.shape[1]
    rows = min(t, 32)
    tiles = l // t
    tile = pl.BlockSpec((nb * t, w), lambda i, j: (i * tiles + j, 0))
    state = pl.BlockSpec((nb, CONV_HIST, w), lambda i, j: (i, 0, 0))
    vec = pl.BlockSpec((1, w), lambda i, j: (0, 0))
    return pl.pallas_call(
        functools.partial(_conv_body, t=t, rows=rows, nb=nb),
        grid=(n // nb, tiles),
        in_specs=[tile, tile, tile, state,
                  pl.BlockSpec((CONV_K, w), lambda i, j: (0, 0)),
                  vec, vec, vec,
                  pl.BlockSpec((w, w), lambda i, j: (0, 0)),
                  vec],
        out_specs=[tile, state],
        out_shape=[jax.ShapeDtypeStruct((n * l, w), out_dtype),
                   jax.ShapeDtypeStruct((n, CONV_HIST, w), F32)],
        scratch_shapes=[pltpu.VMEM((SUBLANES, pl.cdiv(32 + t, SUBLANES) * SUBLANES, w), F32),
                        pltpu.VMEM((nb * t, w), BF16), pltpu.VMEM((CONV_K, SUBLANES, w), F32)],
        compiler_params=_params("parallel", "arbitrary"),
        name="conv_mixer",
    )(a, b, g, hist, w_dw, b_dw.reshape(1, w), ln_g.reshape(1, w), ln_b.reshape(1, w),
      w_pw, b_pw.reshape(1, w))


def _pool_body(u_ref, g_ref, hist_ref, w_ref, scale_ref, y_ref, new_ref, ext_ref, d_ref,
               *, t, pos0, nb):
    ti = pl.program_id(1)
    lead = 16 - POOL_HIST
    pos = pos0 + ti * t + lax.broadcasted_iota(jnp.int32, (t, POOL_GROUP), 0)

    for bi in range(nb):
        seq = slice(bi * t, (bi + 1) * t)

        @pl.when(ti == 0)
        def _():
            ext_ref[lead:16, :] = hist_ref[bi]

        @pl.when(ti > 0)
        def _():
            ext_ref[lead:16, :] = ext_ref[t + lead:t + 16, :]

        ext_ref[16:16 + t, :] = u_ref[seq, :]
        new_ref[bi] = ext_ref[t + lead:t + 16, :]
        for gi, win in enumerate(POOL_WINDOWS):
            sl = slice(gi * POOL_GROUP, (gi + 1) * POOL_GROUP)
            u = u_ref[seq, sl]
            tot = u
            for k in range(1, win):
                tot = tot + ext_ref[16 - k:16 - k + t, sl]
            cnt = jnp.minimum(win, pos + 1).astype(F32)
            d_ref[seq, sl] = (tot / cnt - u).astype(BF16)

    for gi in range(len(POOL_WINDOWS)):
        sl = slice(gi * POOL_GROUP, (gi + 1) * POOL_GROUP)
        y = jnp.dot(d_ref[:, sl], w_ref[gi], preferred_element_type=F32) * scale_ref[:, sl]
        y_ref[:, sl] = (y * _silu(g_ref[:, sl])).astype(y_ref.dtype)


def pool_mixer(u, g, hist, w_pool, scale, n, l, t, nb, pos0, out_dtype):
    w = u.shape[1]
    tiles = l // t
    tile = pl.BlockSpec((nb * t, w), lambda i, j: (i * tiles + j, 0))
    state = pl.BlockSpec((nb, POOL_HIST, w), lambda i, j: (i, 0, 0))
    return pl.pallas_call(
        functools.partial(_pool_body, t=t, pos0=pos0, nb=nb),
        grid=(n // nb, tiles),
        in_specs=[tile, tile, state,
                  pl.BlockSpec(w_pool.shape, lambda i, j: (0, 0, 0)),
                  pl.BlockSpec((1, w), lambda i, j: (0, 0))],
        out_specs=[tile, state],
        out_shape=[jax.ShapeDtypeStruct((n * l, w), out_dtype),
                   jax.ShapeDtypeStruct((n, POOL_HIST, w), F32)],
        scratch_shapes=[pltpu.VMEM((16 + t, w), F32), pltpu.VMEM((nb * t, w), BF16)],
        compiler_params=_params("parallel", "arbitrary"),
        name="pool_mixer",
    )(u, g, hist, w_pool, scale.reshape(1, w))


def _suffix_sum_matrix(blk):
    r = lax.broadcasted_iota(jnp.int32, (blk, blk), 0)
    c = lax.broadcasted_iota(jnp.int32, (blk, blk), 1)
    half = jnp.concatenate([(r >= c).astype(BF16), jnp.ones((blk, blk), BF16)], axis=1)
    return jnp.concatenate([half, half], axis=0)


def _sb_block(nz, tri, carry, mask):
    r = jnp.dot(_sb_log_keep(nz, mask), tri, preferred_element_type=F32)
    return _sb_weights(r, carry, nz, mask)


def _sb_log_keep(nz, mask):
    lg = jnp.minimum(nz, 0.0) - jnp.log(1.0 + jnp.exp2(jnp.abs(nz) * (-LOG2_E)))
    if mask is not None:
        lg = jnp.where(mask, lg, 0.0)
    hi = lg.astype(BF16)
    lo = (lg - hi.astype(F32)).astype(BF16)
    return jnp.concatenate([hi, lo], axis=1)


def _sb_weights(r, carry, nz, mask):
    blk = nz.shape[1]
    a = jnp.exp(r[:, :blk] + carry - nz)
    if mask is not None:
        a = jnp.where(mask, a, 0.0)
    return a, carry + r[:, blk:]


def _sb_body(pt_ref, bias_ref, q_ref, kt_ref, v_ref, g_ref, tri_ref,
             qs_ref, kn_ref, vn_ref, gs_ref, nbias_ref, ckt_hbm, cvt_hbm,
             o_ref, os_ref,
             qq_ref, nz_ref, hl_ref, r_ref, aa_ref, carry_ref, acc_ref,
             kbuf_ref, vbuf_ref, sem_ref, qall_ref, scarry_ref, sacc_ref, pad_ref, ktb_ref, vtb_ref,
             snz_ref, shl_ref, sr_ref, sa_ref, *, ls, pages, n_pages, n_seq, layer):
    tq = SB_PROMPT_BLOCK
    sub = LANES
    halves = tq // sub
    pairs = SB_HEADS // 2
    b = pl.program_id(0)
    i = pl.program_id(1)
    tiles = pl.num_programs(1)
    even_q = lax.broadcasted_iota(jnp.int32, (tq, LANES), 1) < SB_HEAD_DIM
    even_v = lax.broadcasted_iota(jnp.int32, (sub, LANES), 1) < SB_HEAD_DIM
    row = lax.broadcasted_iota(jnp.int32, (2 * tq, sub), 0)
    top = row < tq
    col = lax.broadcasted_iota(jnp.int32, (2 * tq, sub), 1)
    qpos = jnp.where(top, row, row - tq)
    zero = jnp.zeros((), BF16)
    nt = (((1,), (1,)), ((), ()))

    blk = PAGE_SIZE
    rows = ls * SB_HEADS
    w = SB_HEADS * SB_HEAD_DIM
    steps = n_pages // pages
    total = n_seq * steps
    head_mask = (lax.broadcasted_iota(jnp.int32, (SB_HEADS, w), 1) // SB_HEAD_DIM
                 == lax.broadcasted_iota(jnp.int32, (SB_HEADS, w), 0))

    def page_copies(at):
        seq = at // steps
        group = at % steps
        slot = at % SB_SAMPLE_RING
        copies = []
        for r in range(pages):
            page = pt_ref[seq * n_pages + n_pages - 1 - (group * pages + r)]
            copies.append(pltpu.make_async_copy(ckt_hbm.at[layer, page], kbuf_ref.at[slot, r],
                                                sem_ref.at[slot, r]))
            copies.append(pltpu.make_async_copy(cvt_hbm.at[layer, page], vbuf_ref.at[slot, r],
                                                sem_ref.at[slot, pages + r]))
        return copies

    def sample_fetch(at):
        sb = at // steps

        @pl.when(at == 0)
        def _():
            for ahead in range(SB_SAMPLE_RING - 1):
                for copy in page_copies(ahead):
                    copy.start()

        @pl.when(at + (SB_SAMPLE_RING - 1) < total)
        def _():
            for copy in page_copies(at + (SB_SAMPLE_RING - 1)):
                copy.start()

        @pl.when((at < total) & (at % steps == 0))
        def _():
            for t in range(ls):
                qt = qs_ref[pl.ds(sb * ls + t, 1), :] * (-(SB_HEAD_DIM ** -0.5))
                qall_ref[t * SB_HEADS:(t + 1) * SB_HEADS, :] = (
                    jnp.where(head_mask, qt, 0.0).astype(BF16))
            key = lax.broadcasted_iota(jnp.int32, (rows, blk), 1)
            qry = lax.broadcasted_iota(jnp.int32, (rows, blk), 0) // SB_HEADS
            pad_ref[...] = jnp.zeros_like(pad_ref)
            for t in range(ls):
                pad_ref[t:t + 1, :] = kn_ref[pl.ds(sb * ls + t, 1), :]
            nz = lax.dot_general(qall_ref[...], pad_ref[...].astype(BF16), nt,
                                 preferred_element_type=F32)
            for t in range(ls):
                pad_ref[t:t + 1, :] = vn_ref[pl.ds(sb * ls + t, 1), :]
            a, carry = _sb_block(nz + nbias_ref[...], tri_ref[...], jnp.zeros((rows, blk), F32),
                                 key < qry)
            scarry_ref[...] = carry
            sacc_ref[...] = jnp.dot(a.astype(BF16), pad_ref[...].astype(BF16),
                                    preferred_element_type=F32)

        @pl.when(at < total)
        def _():
            for copy in page_copies(at):
                copy.wait()

    def sample_pages(at):
        slot = at % SB_SAMPLE_RING
        tri = tri_ref[...]
        for r in range(pages):
            ktb_ref[:, r * blk:(r + 1) * blk] = kbuf_ref[slot, r].astype(BF16)
            vtb_ref[:, r * blk:(r + 1) * blk] = vbuf_ref[slot, r].astype(BF16)
        snz_ref[...] = (jnp.dot(qall_ref[...], ktb_ref[...], preferred_element_type=F32)
                        + jnp.concatenate([nbias_ref[...]] * pages, axis=1))
        for r in range(pages):
            shl_ref[r * rows:(r + 1) * rows, :] = _sb_log_keep(
                snz_ref[:, r * blk:(r + 1) * blk], None)
        sr_ref[...] = jnp.dot(shl_ref[...], tri, preferred_element_type=F32)
        carry = scarry_ref[...]
        for r in range(pages):
            a, carry = _sb_weights(sr_ref[r * rows:(r + 1) * rows, :], carry,
                                   snz_ref[:, r * blk:(r + 1) * blk], None)
            sa_ref[:, r * blk:(r + 1) * blk] = a.astype(BF16)
        scarry_ref[...] = carry
        sacc_ref[...] += lax.dot_general(sa_ref[...], vtb_ref[...], nt, preferred_element_type=F32)

    def sample_finish(at):
        sb = at // steps

        @pl.when((at < total) & (at % steps == steps - 1))
        def _():
            for t in range(ls):
                o = jnp.where(head_mask, sacc_ref[t * SB_HEADS:(t + 1) * SB_HEADS, :], 0.0)
                o = jnp.sum(o, axis=0, keepdims=True)
                gate = gs_ref[pl.ds(sb * ls + t, 1), :]
                os_ref[pl.ds(sb * ls + t, 1), :] = (o * _silu(gate)).astype(os_ref.dtype)

    for p in range(pairs):
        qp = q_ref[:, p * LANES:(p + 1) * LANES] * (-(SB_HEAD_DIM ** -0.5))
        qq_ref[p] = jnp.concatenate([jnp.where(even_q, qp, 0.0), jnp.where(even_q, 0.0, qp)],
                                    axis=0).astype(BF16)
    carry_ref[...] = jnp.zeros_like(carry_ref)
    acc_ref[...] = jnp.zeros_like(acc_ref)

    def key_block(j, diagonal):
        start = pl.multiple_of(j * tq, tq)
        masks = [col + h * sub < qpos if diagonal else None for h in range(halves)]
        for p in range(pairs):
            kt = kt_ref[0, p * LANES:(p + 1) * LANES, pl.ds(start, tq)].astype(BF16)
            nbias = jnp.where(top, -bias_ref[2 * p], -bias_ref[2 * p + 1])
            nz = jnp.dot(qq_ref[p], kt, preferred_element_type=F32)
            for h in range(halves):
                nz_ref[p, :, h * sub:(h + 1) * sub] = nz[:, h * sub:(h + 1) * sub] + nbias
        for p in range(pairs):
            for h in range(halves):
                hl_ref[p, h] = _sb_log_keep(nz_ref[p, :, h * sub:(h + 1) * sub], masks[h])
        for p in range(pairs):
            for h in range(halves):
                r_ref[p, h] = jnp.dot(hl_ref[p, h], tri_ref[...], preferred_element_type=F32)
        for p in range(pairs):
            carry = carry_ref[p]
            for h in reversed(range(halves)):
                a, carry = _sb_weights(r_ref[p, h], carry, nz_ref[p, :, h * sub:(h + 1) * sub],
                                       masks[h])
                ab = a.astype(BF16)
                aa_ref[p, :, 2 * h * sub:(2 * h + 1) * sub] = ab[:tq]
                aa_ref[p, :, (2 * h + 1) * sub:(2 * h + 2) * sub] = ab[tq:]
            carry_ref[p] = carry
        for p in range(pairs):
            vb = v_ref[pl.ds(start, tq), p * LANES:(p + 1) * LANES]
            vv = []
            for h in range(halves):
                vh = vb[h * sub:(h + 1) * sub]
                vv += [jnp.where(even_v, vh, zero), jnp.where(even_v, zero, vh)]
            acc_ref[p] += jnp.dot(aa_ref[p], jnp.concatenate(vv, axis=0),
                                  preferred_element_type=F32)

    first = b * (tiles * (tiles + 1) // 2) + i * (i + 1) // 2
    sample_fetch(first)
    key_block(i, True)
    sample_pages(first)
    sample_finish(first)

    def step(it, _):
        at = first + 1 + it
        sample_fetch(at)
        key_block(i - 1 - it, False)
        sample_pages(at)
        sample_finish(at)
        return 0

    lax.fori_loop(0, i, step, 0)
    for p in range(pairs):
        sl = slice(p * LANES, (p + 1) * LANES)
        o_ref[:, sl] = (acc_ref[p] * _silu(g_ref[:, sl])).astype(o_ref.dtype)

    @pl.when((b == pl.num_programs(0) - 1) & (i == tiles - 1))
    def _():
        def rest(at, _):
            sample_fetch(at)
            sample_pages(at)
            sample_finish(at)
            return 0

        lax.fori_loop(pl.num_programs(0) * (tiles * (tiles + 1) // 2), total, rest, 0)


def sb_attend(q, kt, v, g, qs, ks_new, vs_new, gs, bias, cache_kt, cache_vt, page_table, layer, ls):
    n, w, l = kt.shape
    m_s = qs.shape[0]
    n_seq, n_pages = page_table.shape
    assert n_seq * (n_pages // SB_SAMPLE_PAGES) >= SB_SAMPLE_RING
    tq = SB_PROMPT_BLOCK
    sub = LANES
    halves = tq // sub
    pairs = SB_HEADS // 2
    tiles = l // tq
    pages = SB_SAMPLE_PAGES
    rows = ls * SB_HEADS
    blk = PAGE_SIZE
    tile = pl.BlockSpec((tq, w), lambda b, i, pt: (b * tiles + i, 0))
    whole = pl.BlockSpec((m_s, w), lambda b, i, pt: (0, 0))
    nbias_rows = jnp.broadcast_to(-jnp.tile(bias, ls)[:, None], (rows, blk)).astype(F32)
    grid_spec = pltpu.PrefetchScalarGridSpec(
        num_scalar_prefetch=1,
        grid=(n, tiles),
        in_specs=[pl.BlockSpec(memory_space=pltpu.SMEM), tile,
                  pl.BlockSpec((1, w, l), lambda b, i, pt: (b, 0, 0)),
                  pl.BlockSpec((l, w), lambda b, i, pt: (b, 0)), tile,
                  pl.BlockSpec((2 * sub, 2 * sub), lambda b, i, pt: (0, 0)),
                  whole, whole, whole, whole,
                  pl.BlockSpec((rows, blk), lambda b, i, pt: (0, 0)),
                  pl.BlockSpec(memory_space=pl.ANY), pl.BlockSpec(memory_space=pl.ANY)],
        out_specs=[tile, whole],
        scratch_shapes=[pltpu.VMEM((pairs, 2 * tq, LANES), BF16),
                        pltpu.VMEM((pairs, 2 * tq, tq), F32),
                        pltpu.VMEM((pairs, halves, 2 * tq, 2 * sub), BF16),
                        pltpu.VMEM((pairs, halves, 2 * tq, 2 * sub), F32),
                        pltpu.VMEM((pairs, tq, halves * 2 * sub), BF16),
                        pltpu.VMEM((pairs, 2 * tq, sub), F32),
                        pltpu.VMEM((pairs, tq, LANES), F32),
                        pltpu.VMEM((SB_SAMPLE_RING, pages, w, blk), F32),
                        pltpu.VMEM((SB_SAMPLE_RING, pages, w, blk), F32),
                        pltpu.SemaphoreType.DMA((SB_SAMPLE_RING, 2 * pages)),
                        pltpu.VMEM((rows, w), BF16), pltpu.VMEM((rows, blk), F32),
                        pltpu.VMEM((rows, w), F32), pltpu.VMEM((blk, w), F32),
                        pltpu.VMEM((w, pages * blk), BF16), pltpu.VMEM((w, pages * blk), BF16),
                        pltpu.VMEM((rows, pages * blk), F32),
                        pltpu.VMEM((pages * rows, 2 * blk), BF16),
                        pltpu.VMEM((pages * rows, 2 * blk), F32),
                        pltpu.VMEM((rows, pages * blk), BF16)],
    )
    return pl.pallas_call(
        functools.partial(_sb_body, ls=ls, pages=pages, n_pages=n_pages, n_seq=n_seq, layer=layer),
        grid_spec=grid_spec,
        out_shape=[jax.ShapeDtypeStruct((n * l, w), BF16), jax.ShapeDtypeStruct((m_s, w), F32)],
        compiler_params=_params("arbitrary", "arbitrary"),
        name="sb_attend",
    )(page_table.reshape(-1), bias, q, kt, v, g, _suffix_sum_matrix(sub),
      qs, ks_new, vs_new, gs, nbias_rows, cache_kt, cache_vt)


def _softmax_rows(s):
    e = jnp.exp(s - jnp.max(s, axis=-1, keepdims=True))
    return e / jnp.sum(e, axis=-1, keepdims=True)


def _mem_body(q_ref, g_ref, mk_ref, mv_ref, o_ref, *, t, nb):
    nt = (((1,), (1,)), ((), ()))
    for bi in range(nb):
        seq = slice(bi * t, (bi + 1) * t)
        for h in range(MEM_HEADS):
            sl = slice(h * MEM_HEAD_DIM, (h + 1) * MEM_HEAD_DIM)
            s = lax.dot_general(q_ref[seq, sl].astype(BF16), mk_ref[bi, :, sl].astype(BF16), nt,
                                preferred_element_type=F32) * (MEM_HEAD_DIM ** -0.5)
            o = jnp.dot(_softmax_rows(s).astype(BF16), mv_ref[bi, :, sl].astype(BF16),
                        preferred_element_type=F32)
            o_ref[seq, sl] = (o * _silu(g_ref[seq, sl])).astype(o_ref.dtype)


def _mem_paired_body(q_ref, g_ref, mk_ref, mv_ref, o_ref, *, t, nb):
    nt = (((1,), (1,)), ((), ()))
    n_rows = mk_ref.shape[2] * mk_ref.shape[3]
    row_head = lax.broadcasted_iota(jnp.int32, (MEM_HEADS * t, n_rows), 1) % MEM_HEADS
    q_head = lax.broadcasted_iota(jnp.int32, (MEM_HEADS * t, n_rows), 0) // t
    for bi in range(nb):
        seq = slice(bi * t, (bi + 1) * t)
        heads = [slice(h * MEM_HEAD_DIM, (h + 1) * MEM_HEAD_DIM) for h in range(MEM_HEADS)]
        q = jnp.concatenate([q_ref[seq, sl] for sl in heads], axis=0).astype(BF16)
        mk = mk_ref[0, bi].reshape(n_rows, MEM_HEAD_DIM).astype(BF16)
        mv = mv_ref[0, bi].reshape(n_rows, MEM_HEAD_DIM).astype(BF16)
        s = lax.dot_general(q, mk, nt, preferred_element_type=F32) * (MEM_HEAD_DIM ** -0.5)
        p = _softmax_rows(jnp.where(row_head == q_head, s, -jnp.inf))
        o = jnp.dot(p.astype(BF16), mv, preferred_element_type=F32)
        for h, sl in enumerate(heads):
            o_ref[seq, sl] = (o[h * t:(h + 1) * t] * _silu(g_ref[seq, sl])).astype(o_ref.dtype)


def mem_attend(q, g, mk, mv, n, l, t, nb, out_dtype, layer=None):
    w = q.shape[1]
    tiles = l // t
    tile = pl.BlockSpec((nb * t, w), lambda b, i: (b * tiles + i, 0))
    if layer is None:
        mem = pl.BlockSpec((nb,) + mk.shape[1:], lambda b, i: (b, 0, 0))
    else:
        mem = pl.BlockSpec((1, nb) + mk.shape[2:], lambda b, i: (layer, b, 0, 0, 0))
    return pl.pallas_call(
        functools.partial(_mem_body if layer is None else _mem_paired_body, t=t, nb=nb),
        grid=(n // nb, tiles),
        in_specs=[tile, tile, mem, mem],
        out_specs=tile,
        out_shape=jax.ShapeDtypeStruct((n * l, w), out_dtype),
        compiler_params=_params("parallel", "arbitrary"),
        name="mem_attend",
    )(q, g, mk, mv)


def _out_proj_body(c_ref, p_ref, s_ref, m_ref, w_ref, pg_ref, x_ref, fg_ref, o_ref, *, final):
    acc = None
    for gi, part in enumerate((c_ref, p_ref, s_ref, m_ref)):
        d = jnp.dot(part[...].astype(BF16), w_ref[gi * GROUP_W:(gi + 1) * GROUP_W, :],
                    preferred_element_type=F32)
        acc = d if acc is None else acc + d
    y = x_ref[...] + _rms(acc, pg_ref[...])
    if final:
        y = _rms(y, fg_ref[...])
    o_ref[...] = y


def out_proj(parts, w, post_g, x, final_g, tm, final):
    m, d = x.shape
    part = pl.BlockSpec((tm, GROUP_W), lambda i: (i, 0))
    vec = pl.BlockSpec((1, d), lambda i: (0, 0))
    row = pl.BlockSpec((tm, d), lambda i: (i, 0))
    return pl.pallas_call(
        functools.partial(_out_proj_body, final=final),
        grid=(m // tm,),
        in_specs=[part, part, part, part, pl.BlockSpec(w.shape, lambda i: (0, 0)), vec, row, vec],
        out_specs=row,
        out_shape=jax.ShapeDtypeStruct((m, d), F32),
        compiler_params=_params("parallel"),
        name="out_proj",
    )(*parts, w, post_g.reshape(1, d), x, final_g.reshape(1, d))


def _project(x, l, lw, tm, keys_on_lanes):
    plan = [(c, "rows", F32) for c in range(len(MIX_GROUPS))]
    if keys_on_lanes:
        plan[K_GROUP] = (K_GROUP, "cols", F32)
        plan[V_GROUP:V_GROUP + 1] = [(V_GROUP, "cols", F32), (V_GROUP, "rows", BF16)]
    z = list(rms_proj(x, lw[0], lw[2], tm, tuple(plan), l))
    v_rows = z.pop(V_GROUP + 1) if keys_on_lanes else z[V_GROUP]
    return z, v_rows


def _mix(x, z, y_s, n, l, conv_hist, pool_hist, pos0, mem_fn, lw, final_g, final, tiles):
    (_, post_g, _, w_dw, b_dw, ln_g, ln_b, w_pw, b_pw, pool_w, pool_scale, _, w_out) = lw
    _, tm_out, t_local, nb, mix_dtype = tiles
    ca, cb, cg, pu, pg, _, _, _, _, mq, mg = z
    y_c, conv_new = conv_mixer(ca, cb, cg, conv_hist, w_dw, b_dw, ln_g, ln_b, w_pw, b_pw, n, l,
                               t_local, nb, mix_dtype)
    y_p, pool_new = pool_mixer(pu, pg, pool_hist, pool_w, pool_scale, n, l, t_local, nb, pos0,
                               mix_dtype)
    y_m = mem_fn(mq, mg, t_local, nb, mix_dtype)
    x = out_proj([y_c, y_p, y_s, y_m], w_out, post_g, x, final_g, tm_out, final)
    return x, conv_new, pool_new


def kernel(x_prompt, x_sample, mem_prompt, cache_k, cache_v, cache_mem_k, cache_mem_v, state_conv,
           state_pool, page_table, pre_g, post_g, w_in, conv_w_dw, conv_b_dw, conv_ln_g, conv_ln_b,
           conv_w_pw, conv_b_pw, pool_w, pool_scale, sb_bias, mem_g, w_mem_kv, w_out, final_g):
    bp, lp, d = x_prompt.shape
    bs, ls, _ = x_sample.shape
    depth = w_in.shape[0]
    n_mem = mem_prompt.shape[1]
    n_phys = cache_k.shape[1]
    past = page_table.shape[1] * PAGE_SIZE
    w = GROUP_W

    xp = x_prompt.reshape(bp * lp, d)
    xs = x_sample.reshape(bs * ls, d)
    mem = mem_prompt.reshape(bp * n_mem, d)
    ckt = jnp.transpose(cache_k, (0, 1, 3, 4, 2)).reshape(depth, n_phys, w, PAGE_SIZE)
    cvt = jnp.transpose(cache_v, (0, 1, 3, 4, 2)).reshape(depth, n_phys, w, PAGE_SIZE)
    pair_shape = (depth, bs, n_mem // 2, 2 * MEM_HEADS, MEM_HEAD_DIM)
    cmk = cache_mem_k.reshape(pair_shape)
    cmv = cache_mem_v.reshape(pair_shape)
    zero_conv = jnp.zeros((bp, CONV_HIST, w), F32)
    zero_pool = jnp.zeros((bp, POOL_HIST, w), F32)
    prompt_tiles = (512, 512, 512, 1, BF16)
    sample_tiles = (bs * ls, bs * ls, ls, 8, F32)

    outs = [[] for _ in range(10)]
    for l in range(depth):
        final = l == depth - 1
        lw = (pre_g[l], post_g[l], w_in[l].astype(BF16), conv_w_dw[l], conv_b_dw[l], conv_ln_g[l],
              conv_ln_b[l], conv_w_pw[l].astype(BF16), conv_b_pw[l], pool_w[l].astype(BF16),
              pool_scale[l], sb_bias[l], w_out[l].astype(BF16))
        mk_p, mv_p = [a.reshape(bp, n_mem, w)
                      for a in rms_proj(mem, mem_g[l], w_mem_kv[l].astype(BF16), 256)]

        def mem_prompt_fn(q, g, t, nb, dtype, mk_p=mk_p, mv_p=mv_p):
            return mem_attend(q, g, mk_p, mv_p, bp, lp, t, nb, dtype)

        def mem_sample_fn(q, g, t, nb, dtype, l=l):
            return mem_attend(q, g, cmk, cmv, bs, ls, t, nb, dtype, layer=l)

        zp, vp_rows = _project(xp, lp, lw, prompt_tiles[0], True)
        zs, _ = _project(xs, ls, lw, sample_tiles[0], False)
        q_at, k_at, v_at, g_at = (MIX_GROUPS.index(name) for name in ("sb_q", "sb_k", "sb_v", "sb_gate"))
        k_p, v_p, k_s, v_s = zp[k_at], zp[v_at], zs[k_at], zs[v_at]
        ys_p, ys_s = sb_attend(zp[q_at], k_p, vp_rows, zp[g_at], zs[q_at], k_s, v_s, zs[g_at],
                               sb_bias[l], ckt, cvt, page_table, l, ls)
        xp, c_p, p_p = _mix(xp, zp, ys_p, bp, lp, zero_conv, zero_pool, 0, mem_prompt_fn, lw,
                            final_g, final, prompt_tiles)
        xs, c_s, p_s = _mix(xs, zs, ys_s, bs, ls, state_conv[l], state_pool[l], past,
                            mem_sample_fn, lw, final_g, final, sample_tiles)
        for lst, a in zip(outs, (k_p, v_p, k_s, v_s, c_p, c_s, p_p, p_s, mk_p, mv_p)):
            lst.append(a)

    k_p, v_p, k_s, v_s, c_p, c_s, p_p, p_s, mk_p, mv_p = [jnp.stack(a) for a in outs]
    heads_first = (depth, bp, SB_HEADS, SB_HEAD_DIM, lp)
    return (xp.reshape(bp, lp, d), xs.reshape(bs, ls, d),
            jnp.transpose(k_p.reshape(heads_first), (0, 1, 4, 2, 3)),
            jnp.transpose(v_p.reshape(heads_first), (0, 1, 4, 2, 3)),
            k_s.reshape(depth, bs, ls, SB_HEADS, SB_HEAD_DIM),
            v_s.reshape(depth, bs, ls, SB_HEADS, SB_HEAD_DIM),
            c_p, c_s, p_p, p_s,
            mk_p.reshape(depth, bp, n_mem, MEM_HEADS, MEM_HEAD_DIM),
            mv_p.reshape(depth, bp, n_mem, MEM_HEADS, MEM_HEAD_DIM))
```

```python
import functools

import jax
import jax.numpy as jnp
from jax import lax
from jax.experimental import pallas as pl
from jax.experimental.pallas import tpu as pltpu

F32 = jnp.float32
BF16 = jnp.bfloat16

EPS = 1e-6
GROUP_W = 512
MIX_GROUPS = ("conv_a", "conv_glu_gate", "conv_silu_gate", "pool_u", "pool_gate", "sb_q", "sb_k",
              "sb_v", "sb_gate", "mem_q", "mem_gate")
BF16_GROUPS = ("sb_q", "mem_q", "conv_silu_gate", "pool_gate", "sb_gate", "mem_gate")
K_GROUP = MIX_GROUPS.index("sb_k")
V_GROUP = MIX_GROUPS.index("sb_v")
CONV_K = 31
CONV_HIST = CONV_K - 1
POOL_WINDOWS = (2, 4, 8, 16)
POOL_GROUP = 128
POOL_HIST = max(POOL_WINDOWS) - 1
SB_HEADS = 8
SB_HEAD_DIM = 64
MEM_HEADS = 4
MEM_HEAD_DIM = 128
PAGE_SIZE = 128
LANES = 128
SUBLANES = 8
SB_PROMPT_BLOCK = 256
SB_SAMPLE_PAGES = 8
SB_SAMPLE_RING = 3
LOG2_E = 1.4426950408889634
VMEM_LIMIT_BYTES = 56 * 1024 * 1024


def _params(*semantics):
    return pltpu.CompilerParams(dimension_semantics=semantics, vmem_limit_bytes=VMEM_LIMIT_BYTES)


def _sigmoid(x):
    return 1.0 / (1.0 + jnp.exp(-x))


def _silu(x):
    return x * _sigmoid(x)


def _rms(x, g):
    return x * lax.rsqrt(jnp.mean(x * x, axis=-1, keepdims=True) + EPS) * g


def _rms_proj_body(x_ref, g_ref, w_ref, *o_refs, plan):
    h = _rms(x_ref[...], g_ref[...]).astype(BF16)
    done = {}
    for (c, kind), o_ref in zip(plan, o_refs):
        if c not in done:
            done = {c: jnp.dot(h, w_ref[:, c * GROUP_W:(c + 1) * GROUP_W],
                               preferred_element_type=F32)}
        y = done[c]
        if kind == "rows":
            o_ref[...] = y.astype(o_ref.dtype)
        else:
            o_ref[0] = y.T


def rms_proj(x, g, w, tm, plan=None, seq_len=None):
    m, d = x.shape
    n = w.shape[1]
    if plan is None:
        plan = tuple((c, "rows", F32) for c in range(n // GROUP_W))
    out_specs, out_shape = [], []
    for _, kind, dtype in plan:
        if kind == "rows":
            out_specs.append(pl.BlockSpec((tm, GROUP_W), lambda i: (i, 0)))
            out_shape.append(jax.ShapeDtypeStruct((m, GROUP_W), dtype))
        else:
            per_seq = seq_len // tm
            out_specs.append(pl.BlockSpec((1, GROUP_W, tm), lambda i: (i // per_seq, 0, i % per_seq)))
            out_shape.append(jax.ShapeDtypeStruct((m // seq_len, GROUP_W, seq_len), dtype))
    return pl.pallas_call(
        functools.partial(_rms_proj_body, plan=tuple((c, kind) for c, kind, _ in plan)),
        grid=(m // tm,),
        in_specs=[pl.BlockSpec((tm, d), lambda i: (i, 0)),
                  pl.BlockSpec((1, d), lambda i: (0, 0)),
                  pl.BlockSpec((d, n), lambda i: (0, 0), pipeline_mode=pl.Buffered(1))],
        out_specs=out_specs,
        out_shape=out_shape,
        compiler_params=_params("parallel"),
        name="rms_proj",
    )(x, g.reshape(1, d), w)


def _conv_body(a_ref, b_ref, g_ref, hist_ref, wdw_ref, bdw_ref, lng_ref, lnb_ref, wpw_ref, bpw_ref,
               y_ref, new_ref, shift_ref, s_ref, taps_ref, *, t, rows, nb):
    ext_ref = shift_ref.at[0]
    ti = pl.program_id(1)
    lead = 32 - CONV_HIST
    n_ext = shift_ref.shape[1]
    tile_rows = min(rows, SUBLANES)
    split = (rows // tile_rows, tile_rows, GROUP_W)

    for bi in range(nb):
        seq = slice(bi * t, (bi + 1) * t)

        @pl.when(ti == 0)
        def _():
            ext_ref[...] = jnp.zeros((n_ext, GROUP_W), F32)
            ext_ref[lead:32, :] = hist_ref[bi]

        @pl.when(ti > 0)
        def _():
            ext_ref[lead:32, :] = ext_ref[t + lead:t + 32, :]

        ext_ref[32:32 + t, :] = a_ref[seq, :] * _sigmoid(b_ref[seq, :])
        ext = ext_ref[...]
        for r in range(1, SUBLANES):
            shift_ref[r] = pltpu.roll(ext, n_ext - r, axis=0)
        for j in range(CONV_K):
            taps_ref[j] = jnp.broadcast_to(wdw_ref[j:j + 1, :], (SUBLANES, GROUP_W))

        new_ref[bi] = ext_ref[t + lead:t + 32, :]

        for c in range(t // rows):
            base = c * rows
            acc = jnp.broadcast_to(bdw_ref[...], (rows, GROUP_W)).reshape(split)
            for j in range(CONV_K):
                blocks, r = divmod(lead + j, SUBLANES)
                start = base + blocks * SUBLANES
                acc = acc + (taps_ref[j, 0:tile_rows, :][None]
                             * shift_ref[r, start:start + rows, :].reshape(split))
            acc = acc.reshape(rows, GROUP_W)
            mu = jnp.mean(acc, axis=-1, keepdims=True)
            xc = acc - mu
            var = jnp.mean(xc * xc, axis=-1, keepdims=True)
            yn = xc * lax.rsqrt(var + EPS) * lng_ref[...] + lnb_ref[...]
            s_ref[bi * t + base:bi * t + base + rows, :] = _silu(yn).astype(BF16)

    y = jnp.dot(s_ref[...], wpw_ref[...], preferred_element_type=F32) + bpw_ref[...]
    y_ref[...] = (y * _silu(g_ref[...].astype(F32))).astype(y_ref.dtype)


def conv_mixer(a, b, g, hist, w_dw, b_dw, ln_g, ln_b, w_pw, b_pw, n, l, t, nb, out_dtype):
    w = a.shape[1]
    rows = min(t, 32)
    tiles = l // t
    tile = pl.BlockSpec((nb * t, w), lambda i, j: (i * tiles + j, 0))
    state = pl.BlockSpec((nb, CONV_HIST, w), lambda i, j: (i, 0, 0))
    vec = pl.BlockSpec((1, w), lambda i, j: (0, 0))
    return pl.pallas_call(
        functools.partial(_conv_body, t=t, rows=rows, nb=nb),
        grid=(n // nb, tiles),
        in_specs=[tile, tile, tile, state,
                  pl.BlockSpec((CONV_K, w), lambda i, j: (0, 0)),
                  vec, vec, vec,
                  pl.BlockSpec((w, w), lambda i, j: (0, 0)),
                  vec],
        out_specs=[tile, state],
        out_shape=[jax.ShapeDtypeStruct((n * l, w), out_dtype),
                   jax.ShapeDtypeStruct((n, CONV_HIST, w), F32)],
        scratch_shapes=[pltpu.VMEM((SUBLANES, pl.cdiv(32 + t, SUBLANES) * SUBLANES, w), F32),
                        pltpu.VMEM((nb * t, w), BF16), pltpu.VMEM((CONV_K, SUBLANES, w), F32)],
        compiler_params=_params("parallel", "arbitrary"),
        name="conv_mixer",
    )(a, b, g, hist, w_dw, b_dw.reshape(1, w), ln_g.reshape(1, w), ln_b.reshape(1, w),
      w_pw, b_pw.reshape(1, w))


def _pool_body(u_ref, g_ref, hist_ref, w_ref, scale_ref, y_ref, new_ref, ext_ref, d_ref,
               *, t, pos0, nb):
    ti = pl.program_id(1)
    lead = 16 - POOL_HIST
    pos = pos0 + ti * t + lax.broadcasted_iota(jnp.int32, (t, POOL_GROUP), 0)

    for bi in range(nb):
        seq = slice(bi * t, (bi + 1) * t)

        @pl.when(ti == 0)
        def _():
            ext_ref[lead:16, :] = hist_ref[bi]

        @pl.when(ti > 0)
        def _():
            ext_ref[lead:16, :] = ext_ref[t + lead:t + 16, :]

        ext_ref[16:16 + t, :] = u_ref[seq, :]
        new_ref[bi] = ext_ref[t + lead:t + 16, :]
        for gi, win in enumerate(POOL_WINDOWS):
            sl = slice(gi * POOL_GROUP, (gi + 1) * POOL_GROUP)
            u = u_ref[seq, sl]
            tot = u
            for k in range(1, win):
                tot = tot + ext_ref[16 - k:16 - k + t, sl]
            cnt = jnp.minimum(win, pos + 1).astype(F32)
            d_ref[seq, sl] = (tot / cnt - u).astype(BF16)

    for gi in range(len(POOL_WINDOWS)):
        sl = slice(gi * POOL_GROUP, (gi + 1) * POOL_GROUP)
        y = jnp.dot(d_ref[:, sl], w_ref[gi], preferred_element_type=F32) * scale_ref[:, sl]
        y_ref[:, sl] = (y * _silu(g_ref[:, sl].astype(F32))).astype(y_ref.dtype)


def pool_mixer(u, g, hist, w_pool, scale, n, l, t, nb, pos0, out_dtype):
    w = u.shape[1]
    tiles = l // t
    tile = pl.BlockSpec((nb * t, w), lambda i, j: (i * tiles + j, 0))
    state = pl.BlockSpec((nb, POOL_HIST, w), lambda i, j: (i, 0, 0))
    return pl.pallas_call(
        functools.partial(_pool_body, t=t, pos0=pos0, nb=nb),
        grid=(n // nb, tiles),
        in_specs=[tile, tile, state,
                  pl.BlockSpec(w_pool.shape, lambda i, j: (0, 0, 0)),
                  pl.BlockSpec((1, w), lambda i, j: (0, 0))],
        out_specs=[tile, state],
        out_shape=[jax.ShapeDtypeStruct((n * l, w), out_dtype),
                   jax.ShapeDtypeStruct((n, POOL_HIST, w), F32)],
        scratch_shapes=[pltpu.VMEM((16 + t, w), F32), pltpu.VMEM((nb * t, w), BF16)],
        compiler_params=_params("parallel", "arbitrary"),
        name="pool_mixer",
    )(u, g, hist, w_pool, scale.reshape(1, w))


def _suffix_sum_matrix(blk):
    r = lax.broadcasted_iota(jnp.int32, (blk, blk), 0)
    c = lax.broadcasted_iota(jnp.int32, (blk, blk), 1)
    half = jnp.concatenate([(r >= c).astype(BF16), jnp.ones((blk, blk), BF16)], axis=1)
    return jnp.concatenate([half, half], axis=0)


def _sb_block(nz, tri, carry, mask):
    r = jnp.dot(_sb_log_keep(nz, mask), tri, preferred_element_type=F32)
    return _sb_weights(r, carry, nz, mask)


def _sb_log_keep(nz, mask):
    lg = jnp.minimum(nz, 0.0) - jnp.log(1.0 + jnp.exp2(jnp.abs(nz) * (-LOG2_E)))
    if mask is not None:
        lg = jnp.where(mask, lg, 0.0)
    hi = lg.astype(BF16)
    lo = (lg - hi.astype(F32)).astype(BF16)
    return jnp.concatenate([hi, lo], axis=1)


def _sb_weights(r, carry, nz, mask):
    blk = nz.shape[1]
    a = jnp.exp(r[:, :blk] + carry - nz)
    if mask is not None:
        a = jnp.where(mask, a, 0.0)
    return a, carry + r[:, blk:]


def _sb_body(pt_ref, bias_ref, q_ref, kt_ref, v_ref, g_ref, tri_ref,
             qs_ref, kn_ref, vn_ref, gs_ref, nbias_ref, ckt_hbm, cvt_hbm,
             o_ref, os_ref,
             qq_ref, nz_ref, hl_ref, r_ref, aa_ref, carry_ref, acc_ref,
             kbuf_ref, vbuf_ref, sem_ref, qall_ref, scarry_ref, sacc_ref, pad_ref, ktb_ref, vtb_ref,
             snz_ref, shl_ref, sr_ref, sa_ref, *, ls, pages, n_pages, n_seq, layer):
    tq = SB_PROMPT_BLOCK
    sub = LANES
    halves = tq // sub
    pairs = SB_HEADS // 2
    b = pl.program_id(0)
    i = pl.program_id(1)
    tiles = pl.num_programs(1)
    even_q = lax.broadcasted_iota(jnp.int32, (tq, LANES), 1) < SB_HEAD_DIM
    even_v = lax.broadcasted_iota(jnp.int32, (sub, LANES), 1) < SB_HEAD_DIM
    row = lax.broadcasted_iota(jnp.int32, (2 * tq, sub), 0)
    top = row < tq
    col = lax.broadcasted_iota(jnp.int32, (2 * tq, sub), 1)
    qpos = jnp.where(top, row, row - tq)
    zero = jnp.zeros((), BF16)
    nt = (((1,), (1,)), ((), ()))

    blk = PAGE_SIZE
    rows = ls * SB_HEADS
    w = SB_HEADS * SB_HEAD_DIM
    steps = n_pages // pages
    total = n_seq * steps
    head_mask = (lax.broadcasted_iota(jnp.int32, (SB_HEADS, w), 1) // SB_HEAD_DIM
                 == lax.broadcasted_iota(jnp.int32, (SB_HEADS, w), 0))

    def page_copies(at):
        seq = at // steps
        group = at % steps
        slot = at % SB_SAMPLE_RING
        copies = []
        for r in range(pages):
            page = pt_ref[seq * n_pages + n_pages - 1 - (group * pages + r)]
            copies.append(pltpu.make_async_copy(ckt_hbm.at[layer, page], kbuf_ref.at[slot, r],
                                                sem_ref.at[slot, r]))
            copies.append(pltpu.make_async_copy(cvt_hbm.at[layer, page], vbuf_ref.at[slot, r],
                                                sem_ref.at[slot, pages + r]))
        return copies

    def sample_fetch(at):
        sb = at // steps

        @pl.when(at == 0)
        def _():
            for ahead in range(SB_SAMPLE_RING - 1):
                for copy in page_copies(ahead):
                    copy.start()

        @pl.when(at + (SB_SAMPLE_RING - 1) < total)
        def _():
            for copy in page_copies(at + (SB_SAMPLE_RING - 1)):
                copy.start()

        @pl.when((at < total) & (at % steps == 0))
        def _():
            for t in range(ls):
                qt = qs_ref[pl.ds(sb * ls + t, 1), :] * (-(SB_HEAD_DIM ** -0.5))
                qall_ref[t * SB_HEADS:(t + 1) * SB_HEADS, :] = (
                    jnp.where(head_mask, qt, 0.0).astype(BF16))
            key = lax.broadcasted_iota(jnp.int32, (rows, blk), 1)
            qry = lax.broadcasted_iota(jnp.int32, (rows, blk), 0) // SB_HEADS
            pad_ref[...] = jnp.zeros_like(pad_ref)
            for t in range(ls):
                pad_ref[t:t + 1, :] = kn_ref[pl.ds(sb * ls + t, 1), :]
            nz = lax.dot_general(qall_ref[...], pad_ref[...].astype(BF16), nt,
                                 preferred_element_type=F32)
            for t in range(ls):
                pad_ref[t:t + 1, :] = vn_ref[pl.ds(sb * ls + t, 1), :]
            a, carry = _sb_block(nz + nbias_ref[...], tri_ref[...], jnp.zeros((rows, blk), F32),
                                 key < qry)
            scarry_ref[...] = carry
            sacc_ref[...] = jnp.dot(a.astype(BF16), pad_ref[...].astype(BF16),
                                    preferred_element_type=F32)

        @pl.when(at < total)
        def _():
            for copy in page_copies(at):
                copy.wait()

    def sample_pages(at):
        slot = at % SB_SAMPLE_RING
        tri = tri_ref[...]
        for r in range(pages):
            ktb_ref[:, r * blk:(r + 1) * blk] = kbuf_ref[slot, r].astype(BF16)
            vtb_ref[:, r * blk:(r + 1) * blk] = vbuf_ref[slot, r].astype(BF16)
        snz_ref[...] = (jnp.dot(qall_ref[...], ktb_ref[...], preferred_element_type=F32)
                        + jnp.concatenate([nbias_ref[...]] * pages, axis=1))
        for r in range(pages):
            shl_ref[r * rows:(r + 1) * rows, :] = _sb_log_keep(
                snz_ref[:, r * blk:(r + 1) * blk], None)
        sr_ref[...] = jnp.dot(shl_ref[...], tri, preferred_element_type=F32)
        carry = scarry_ref[...]
        for r in range(pages):
            a, carry = _sb_weights(sr_ref[r * rows:(r + 1) * rows, :], carry,
                                   snz_ref[:, r * blk:(r + 1) * blk], None)
            sa_ref[:, r * blk:(r + 1) * blk] = a.astype(BF16)
        scarry_ref[...] = carry
        sacc_ref[...] += lax.dot_general(sa_ref[...], vtb_ref[...], nt, preferred_element_type=F32)

    def sample_finish(at):
        sb = at // steps

        @pl.when((at < total) & (at % steps == steps - 1))
        def _():
            for t in range(ls):
                o = jnp.where(head_mask, sacc_ref[t * SB_HEADS:(t + 1) * SB_HEADS, :], 0.0)
                o = jnp.sum(o, axis=0, keepdims=True)
                gate = gs_ref[pl.ds(sb * ls + t, 1), :]
                os_ref[pl.ds(sb * ls + t, 1), :] = (o * _silu(gate)).astype(os_ref.dtype)

    for p in range(pairs):
        qp = q_ref[:, p * LANES:(p + 1) * LANES] * (-(SB_HEAD_DIM ** -0.5))
        qq_ref[p] = jnp.concatenate([jnp.where(even_q, qp, 0.0), jnp.where(even_q, 0.0, qp)],
                                    axis=0).astype(BF16)
    carry_ref[...] = jnp.zeros_like(carry_ref)
    acc_ref[...] = jnp.zeros_like(acc_ref)

    def key_block(j, diagonal):
        start = pl.multiple_of(j * tq, tq)
        masks = [col + h * sub < qpos if diagonal else None for h in range(halves)]
        for p in range(pairs):
            kt = kt_ref[0, p * LANES:(p + 1) * LANES, pl.ds(start, tq)].astype(BF16)
            nbias = jnp.where(top, -bias_ref[2 * p], -bias_ref[2 * p + 1])
            nz = jnp.dot(qq_ref[p], kt, preferred_element_type=F32)
            for h in range(halves):
                nz_ref[p, :, h * sub:(h + 1) * sub] = nz[:, h * sub:(h + 1) * sub] + nbias
        for p in range(pairs):
            for h in range(halves):
                hl_ref[p, h] = _sb_log_keep(nz_ref[p, :, h * sub:(h + 1) * sub], masks[h])
        for p in range(pairs):
            for h in range(halves):
                r_ref[p, h] = jnp.dot(hl_ref[p, h], tri_ref[...], preferred_element_type=F32)
        for p in range(pairs):
            carry = carry_ref[p]
            for h in reversed(range(halves)):
                a, carry = _sb_weights(r_ref[p, h], carry, nz_ref[p, :, h * sub:(h + 1) * sub],
                                       masks[h])
                ab = a.astype(BF16)
                aa_ref[p, :, 2 * h * sub:(2 * h + 1) * sub] = ab[:tq]
                aa_ref[p, :, (2 * h + 1) * sub:(2 * h + 2) * sub] = ab[tq:]
            carry_ref[p] = carry
        for p in range(pairs):
            vb = v_ref[pl.ds(start, tq), p * LANES:(p + 1) * LANES]
            vv = []
            for h in range(halves):
                vh = vb[h * sub:(h + 1) * sub]
                vv += [jnp.where(even_v, vh, zero), jnp.where(even_v, zero, vh)]
            acc_ref[p] += jnp.dot(aa_ref[p], jnp.concatenate(vv, axis=0),
                                  preferred_element_type=F32)

    first = b * (tiles * (tiles + 1) // 2) + i * (i + 1) // 2
    sample_fetch(first)
    key_block(i, True)
    sample_pages(first)
    sample_finish(first)

    def step(it, _):
        at = first + 1 + it
        sample_fetch(at)
        key_block(i - 1 - it, False)
        sample_pages(at)
        sample_finish(at)
        return 0

    lax.fori_loop(0, i, step, 0)
    for p in range(pairs):
        sl = slice(p * LANES, (p + 1) * LANES)
        o_ref[:, sl] = (acc_ref[p] * _silu(g_ref[:, sl].astype(F32))).astype(o_ref.dtype)

    @pl.when((b == pl.num_programs(0) - 1) & (i == tiles - 1))
    def _():
        def rest(at, _):
            sample_fetch(at)
            sample_pages(at)
            sample_finish(at)
            return 0

        lax.fori_loop(pl.num_programs(0) * (tiles * (tiles + 1) // 2), total, rest, 0)


def sb_attend(q, kt, v, g, qs, ks_new, vs_new, gs, bias, cache_kt, cache_vt, page_table, layer, ls):
    n, w, l = kt.shape
    m_s = qs.shape[0]
    n_seq, n_pages = page_table.shape
    assert n_seq * (n_pages // SB_SAMPLE_PAGES) >= SB_SAMPLE_RING
    tq = SB_PROMPT_BLOCK
    sub = LANES
    halves = tq // sub
    pairs = SB_HEADS // 2
    tiles = l // tq
    pages = SB_SAMPLE_PAGES
    rows = ls * SB_HEADS
    blk = PAGE_SIZE
    tile = pl.BlockSpec((tq, w), lambda b, i, pt: (b * tiles + i, 0))
    whole = pl.BlockSpec((m_s, w), lambda b, i, pt: (0, 0))
    nbias_rows = jnp.broadcast_to(-jnp.tile(bias, ls)[:, None], (rows, blk)).astype(F32)
    grid_spec = pltpu.PrefetchScalarGridSpec(
        num_scalar_prefetch=1,
        grid=(n, tiles),
        in_specs=[pl.BlockSpec(memory_space=pltpu.SMEM), tile,
                  pl.BlockSpec((1, w, l), lambda b, i, pt: (b, 0, 0)),
                  pl.BlockSpec((l, w), lambda b, i, pt: (b, 0)), tile,
                  pl.BlockSpec((2 * sub, 2 * sub), lambda b, i, pt: (0, 0)),
                  whole, whole, whole, whole,
                  pl.BlockSpec((rows, blk), lambda b, i, pt: (0, 0)),
                  pl.BlockSpec(memory_space=pl.ANY), pl.BlockSpec(memory_space=pl.ANY)],
        out_specs=[tile, whole],
        scratch_shapes=[pltpu.VMEM((pairs, 2 * tq, LANES), BF16),
                        pltpu.VMEM((pairs, 2 * tq, tq), F32),
                        pltpu.VMEM((pairs, halves, 2 * tq, 2 * sub), BF16),
                        pltpu.VMEM((pairs, halves, 2 * tq, 2 * sub), F32),
                        pltpu.VMEM((pairs, tq, halves * 2 * sub), BF16),
                        pltpu.VMEM((pairs, 2 * tq, sub), F32),
                        pltpu.VMEM((pairs, tq, LANES), F32),
                        pltpu.VMEM((SB_SAMPLE_RING, pages, w, blk), F32),
                        pltpu.VMEM((SB_SAMPLE_RING, pages, w, blk), F32),
                        pltpu.SemaphoreType.DMA((SB_SAMPLE_RING, 2 * pages)),
                        pltpu.VMEM((rows, w), BF16), pltpu.VMEM((rows, blk), F32),
                        pltpu.VMEM((rows, w), F32), pltpu.VMEM((blk, w), F32),
                        pltpu.VMEM((w, pages * blk), BF16), pltpu.VMEM((w, pages * blk), BF16),
                        pltpu.VMEM((rows, pages * blk), F32),
                        pltpu.VMEM((pages * rows, 2 * blk), BF16),
                        pltpu.VMEM((pages * rows, 2 * blk), F32),
                        pltpu.VMEM((rows, pages * blk), BF16)],
    )
    return pl.pallas_call(
        functools.partial(_sb_body, ls=ls, pages=pages, n_pages=n_pages, n_seq=n_seq, layer=layer),
        grid_spec=grid_spec,
        out_shape=[jax.ShapeDtypeStruct((n * l, w), BF16), jax.ShapeDtypeStruct((m_s, w), F32)],
        compiler_params=_params("arbitrary", "arbitrary"),
        name="sb_attend",
    )(page_table.reshape(-1), bias, q, kt, v, g, _suffix_sum_matrix(sub),
      qs, ks_new, vs_new, gs, nbias_rows, cache_kt, cache_vt)


def _softmax_rows(s):
    e = jnp.exp(s - jnp.max(s, axis=-1, keepdims=True))
    return e / jnp.sum(e, axis=-1, keepdims=True)


def _mem_body(q_ref, g_ref, mk_ref, mv_ref, o_ref, *, t, nb):
    nt = (((1,), (1,)), ((), ()))
    for bi in range(nb):
        seq = slice(bi * t, (bi + 1) * t)
        for h in range(MEM_HEADS):
            sl = slice(h * MEM_HEAD_DIM, (h + 1) * MEM_HEAD_DIM)
            s = lax.dot_general(q_ref[seq, sl].astype(BF16), mk_ref[bi, :, sl].astype(BF16), nt,
                                preferred_element_type=F32) * (MEM_HEAD_DIM ** -0.5)
            o = jnp.dot(_softmax_rows(s).astype(BF16), mv_ref[bi, :, sl].astype(BF16),
                        preferred_element_type=F32)
            o_ref[seq, sl] = (o * _silu(g_ref[seq, sl].astype(F32))).astype(o_ref.dtype)


def _mem_paired_body(q_ref, g_ref, mk_ref, mv_ref, o_ref, *, t, nb):
    nt = (((1,), (1,)), ((), ()))
    n_rows = mk_ref.shape[2] * mk_ref.shape[3]
    row_head = lax.broadcasted_iota(jnp.int32, (MEM_HEADS * t, n_rows), 1) % MEM_HEADS
    q_head = lax.broadcasted_iota(jnp.int32, (MEM_HEADS * t, n_rows), 0) // t
    for bi in range(nb):
        seq = slice(bi * t, (bi + 1) * t)
        heads = [slice(h * MEM_HEAD_DIM, (h + 1) * MEM_HEAD_DIM) for h in range(MEM_HEADS)]
        q = jnp.concatenate([q_ref[seq, sl] for sl in heads], axis=0).astype(BF16)
        mk = mk_ref[0, bi].reshape(n_rows, MEM_HEAD_DIM).astype(BF16)
        mv = mv_ref[0, bi].reshape(n_rows, MEM_HEAD_DIM).astype(BF16)
        s = lax.dot_general(q, mk, nt, preferred_element_type=F32) * (MEM_HEAD_DIM ** -0.5)
        p = _softmax_rows(jnp.where(row_head == q_head, s, -jnp.inf))
        o = jnp.dot(p.astype(BF16), mv, preferred_element_type=F32)
        for h, sl in enumerate(heads):
            o_ref[seq, sl] = (o[h * t:(h + 1) * t] * _silu(g_ref[seq, sl])).astype(o_ref.dtype)


def mem_attend(q, g, mk, mv, n, l, t, nb, out_dtype, layer=None):
    w = q.shape[1]
    tiles = l // t
    tile = pl.BlockSpec((nb * t, w), lambda b, i: (b * tiles + i, 0))
    if layer is None:
        mem = pl.BlockSpec((nb,) + mk.shape[1:], lambda b, i: (b, 0, 0))
    else:
        mem = pl.BlockSpec((1, nb) + mk.shape[2:], lambda b, i: (layer, b, 0, 0, 0))
    return pl.pallas_call(
        functools.partial(_mem_body if layer is None else _mem_paired_body, t=t, nb=nb),
        grid=(n // nb, tiles),
        in_specs=[tile, tile, mem, mem],
        out_specs=tile,
        out_shape=jax.ShapeDtypeStruct((n * l, w), out_dtype),
        compiler_params=_params("parallel", "arbitrary"),
        name="mem_attend",
    )(q, g, mk, mv)


def _out_proj_body(c_ref, p_ref, s_ref, m_ref, w_ref, pg_ref, x_ref, fg_ref, o_ref, *, final):
    acc = None
    for gi, part in enumerate((c_ref, p_ref, s_ref, m_ref)):
        d = jnp.dot(part[...].astype(BF16), w_ref[gi * GROUP_W:(gi + 1) * GROUP_W, :],
                    preferred_element_type=F32)
        acc = d if acc is None else acc + d
    y = x_ref[...] + _rms(acc, pg_ref[...])
    if final:
        y = _rms(y, fg_ref[...])
    o_ref[...] = y


def out_proj(parts, w, post_g, x, final_g, tm, final):
    m, d = x.shape
    part = pl.BlockSpec((tm, GROUP_W), lambda i: (i, 0))
    vec = pl.BlockSpec((1, d), lambda i: (0, 0))
    row = pl.BlockSpec((tm, d), lambda i: (i, 0))
    return pl.pallas_call(
        functools.partial(_out_proj_body, final=final),
        grid=(m // tm,),
        in_specs=[part, part, part, part, pl.BlockSpec(w.shape, lambda i: (0, 0)), vec, row, vec],
        out_specs=row,
        out_shape=jax.ShapeDtypeStruct((m, d), F32),
        compiler_params=_params("parallel"),
        name="out_proj",
    )(*parts, w, post_g.reshape(1, d), x, final_g.reshape(1, d))


def _project(x, l, lw, tm, keys_on_lanes):
    plan = [(c, "rows", F32) for c in range(len(MIX_GROUPS))]
    if keys_on_lanes:
        for c, name in enumerate(MIX_GROUPS):
            if name in BF16_GROUPS:
                plan[c] = (c, "rows", BF16)
        plan[K_GROUP] = (K_GROUP, "cols", F32)
        plan[V_GROUP:V_GROUP + 1] = [(V_GROUP, "cols", F32), (V_GROUP, "rows", BF16)]
    z = list(rms_proj(x, lw[0], lw[2], tm, tuple(plan), l))
    v_rows = z.pop(V_GROUP + 1) if keys_on_lanes else z[V_GROUP]
    return z, v_rows


def _mix(x, z, y_s, n, l, conv_hist, pool_hist, pos0, mem_fn, lw, final_g, final, tiles):
    (_, post_g, _, w_dw, b_dw, ln_g, ln_b, w_pw, b_pw, pool_w, pool_scale, _, w_out) = lw
    _, tm_out, t_local, nb, mix_dtype = tiles
    ca, cb, cg, pu, pg, _, _, _, _, mq, mg = z
    y_c, conv_new = conv_mixer(ca, cb, cg, conv_hist, w_dw, b_dw, ln_g, ln_b, w_pw, b_pw, n, l,
                               t_local, nb, mix_dtype)
    y_p, pool_new = pool_mixer(pu, pg, pool_hist, pool_w, pool_scale, n, l, t_local, nb, pos0,
                               mix_dtype)
    y_m = mem_fn(mq, mg, t_local, nb, mix_dtype)
    x = out_proj([y_c, y_p, y_s, y_m], w_out, post_g, x, final_g, tm_out, final)
    return x, conv_new, pool_new


def kernel(x_prompt, x_sample, mem_prompt, cache_k, cache_v, cache_mem_k, cache_mem_v, state_conv,
           state_pool, page_table, pre_g, post_g, w_in, conv_w_dw, conv_b_dw, conv_ln_g, conv_ln_b,
           conv_w_pw, conv_b_pw, pool_w, pool_scale, sb_bias, mem_g, w_mem_kv, w_out, final_g):
    bp, lp, d = x_prompt.shape
    bs, ls, _ = x_sample.shape
    depth = w_in.shape[0]
    n_mem = mem_prompt.shape[1]
    n_phys = cache_k.shape[1]
    past = page_table.shape[1] * PAGE_SIZE
    w = GROUP_W

    xp = x_prompt.reshape(bp * lp, d)
    xs = x_sample.reshape(bs * ls, d)
    mem = mem_prompt.reshape(bp * n_mem, d)
    ckt = jnp.transpose(cache_k, (0, 1, 3, 4, 2)).reshape(depth, n_phys, w, PAGE_SIZE)
    cvt = jnp.transpose(cache_v, (0, 1, 3, 4, 2)).reshape(depth, n_phys, w, PAGE_SIZE)
    pair_shape = (depth, bs, n_mem // 2, 2 * MEM_HEADS, MEM_HEAD_DIM)
    cmk = cache_mem_k.reshape(pair_shape)
    cmv = cache_mem_v.reshape(pair_shape)
    zero_conv = jnp.zeros((bp, CONV_HIST, w), F32)
    zero_pool = jnp.zeros((bp, POOL_HIST, w), F32)
    prompt_tiles = (512, 512, 512, 1, BF16)
    sample_tiles = (bs * ls, bs * ls, ls, 8, F32)

    outs = [[] for _ in range(10)]
    for l in range(depth):
        final = l == depth - 1
        lw = (pre_g[l], post_g[l], w_in[l].astype(BF16), conv_w_dw[l], conv_b_dw[l], conv_ln_g[l],
              conv_ln_b[l], conv_w_pw[l].astype(BF16), conv_b_pw[l], pool_w[l].astype(BF16),
              pool_scale[l], sb_bias[l], w_out[l].astype(BF16))
        mk_p, mv_p = [a.reshape(bp, n_mem, w)
                      for a in rms_proj(mem, mem_g[l], w_mem_kv[l].astype(BF16), 256)]

        def mem_prompt_fn(q, g, t, nb, dtype, mk_p=mk_p, mv_p=mv_p):
            return mem_attend(q, g, mk_p, mv_p, bp, lp, t, nb, dtype)

        def mem_sample_fn(q, g, t, nb, dtype, l=l):
            return mem_attend(q, g, cmk, cmv, bs, ls, t, nb, dtype, layer=l)

        zp, vp_rows = _project(xp, lp, lw, prompt_tiles[0], True)
        zs, _ = _project(xs, ls, lw, sample_tiles[0], False)
        q_at, k_at, v_at, g_at = (MIX_GROUPS.index(name) for name in ("sb_q", "sb_k", "sb_v", "sb_gate"))
        k_p, v_p, k_s, v_s = zp[k_at], zp[v_at], zs[k_at], zs[v_at]
        ys_p, ys_s = sb_attend(zp[q_at], k_p, vp_rows, zp[g_at], zs[q_at], k_s, v_s, zs[g_at],
                               sb_bias[l], ckt, cvt, page_table, l, ls)
        xp, c_p, p_p = _mix(xp, zp, ys_p, bp, lp, zero_conv, zero_pool, 0, mem_prompt_fn, lw,
                            final_g, final, prompt_tiles)
        xs, c_s, p_s = _mix(xs, zs, ys_s, bs, ls, state_conv[l], state_pool[l], past,
                            mem_sample_fn, lw, final_g, final, sample_tiles)
        for lst, a in zip(outs, (k_p, v_p, k_s, v_s, c_p, c_s, p_p, p_s, mk_p, mv_p)):
            lst.append(a)

    k_p, v_p, k_s, v_s, c_p, c_s, p_p, p_s, mk_p, mv_p = [jnp.stack(a) for a in outs]
    heads_first = (depth, bp, SB_HEADS, SB_HEAD_DIM, lp)
    return (xp.reshape(bp, lp, d), xs.reshape(bs, ls, d),
            jnp.transpose(k_p.reshape(heads_first), (0, 1, 4, 2, 3)),
            jnp.transpose(v_p.reshape(heads_first), (0, 1, 4, 2, 3)),
            k_s.reshape(depth, bs, ls, SB_HEADS, SB_HEAD_DIM),
            v_s.reshape(depth, bs, ls, SB_HEADS, SB_HEAD_DIM),
            c_p, c_s, p_p, p_s,
            mk_p.reshape(depth, bp, n_mem, MEM_HEADS, MEM_HEAD_DIM),
            mv_p.reshape(depth, bp, n_mem, MEM_HEADS, MEM_HEAD_DIM))
```

```python
import functools

import jax
import jax.numpy as jnp
from jax import lax
from jax.experimental import pallas as pl
from jax.experimental.pallas import tpu as pltpu

F32 = jnp.float32
BF16 = jnp.bfloat16

EPS = 1e-6
GROUP_W = 512
MIX_GROUPS = ("conv_a", "conv_glu_gate", "conv_silu_gate", "pool_u", "pool_gate", "sb_q", "sb_k",
              "sb_v", "sb_gate", "mem_q", "mem_gate")
BF16_GROUPS = ("sb_q", "mem_q")
K_GROUP = MIX_GROUPS.index("sb_k")
V_GROUP = MIX_GROUPS.index("sb_v")
CONV_K = 31
CONV_HIST = CONV_K - 1
POOL_WINDOWS = (2, 4, 8, 16)
POOL_GROUP = 128
POOL_HIST = max(POOL_WINDOWS) - 1
SB_HEADS = 8
SB_HEAD_DIM = 64
MEM_HEADS = 4
MEM_HEAD_DIM = 128
PAGE_SIZE = 128
LANES = 128
SUBLANES = 8
SB_PROMPT_BLOCK = 256
SB_SAMPLE_PAGES = 8
SB_SAMPLE_RING = 3
LOG2_E = 1.4426950408889634
VMEM_LIMIT_BYTES = 56 * 1024 * 1024


def _params(*semantics):
    return pltpu.CompilerParams(dimension_semantics=semantics, vmem_limit_bytes=VMEM_LIMIT_BYTES)


def _sigmoid(x):
    return 1.0 / (1.0 + jnp.exp(-x))


def _silu(x):
    return x * _sigmoid(x)


def _rms(x, g):
    return x * lax.rsqrt(jnp.mean(x * x, axis=-1, keepdims=True) + EPS) * g


def _rms_proj_body(x_ref, g_ref, w_ref, *rest, plan, layer):
    o_refs = rest[len(rest) - len(plan):]
    h = _rms(x_ref[...], g_ref[...]).astype(BF16)
    done = {}
    for (c, kind), o_ref in zip(plan, o_refs):
        if c not in done:
            done = {c: jnp.dot(h, w_ref[:, c * GROUP_W:(c + 1) * GROUP_W],
                               preferred_element_type=F32)}
        y = done[c]
        if kind == "rows":
            o_ref[...] = y.astype(o_ref.dtype)
        elif o_ref.shape[0] == 1:
            o_ref[0, 0] = y.T
        else:
            o_ref[...] = jnp.zeros(o_ref.shape, o_ref.dtype)
            o_ref[layer, 0] = y.T


def rms_proj(x, g, w, tm, plan=None, seq_len=None, layer=0, depth=1, stacked=()):
    m, d = x.shape
    n = w.shape[1]
    if plan is None:
        plan = tuple((c, "rows", F32) for c in range(n // GROUP_W))
    out_specs, out_shape, cols_at = [], [], []
    for k, (_, kind, dtype) in enumerate(plan):
        if kind == "rows":
            out_specs.append(pl.BlockSpec((tm, GROUP_W), lambda i: (i, 0)))
            out_shape.append(jax.ShapeDtypeStruct((m, GROUP_W), dtype))
        else:
            per_seq = seq_len // tm
            if stacked:
                spec = pl.BlockSpec((1, 1, GROUP_W, tm),
                                    lambda i: (layer, i // per_seq, 0, i % per_seq))
            else:
                spec = pl.BlockSpec((depth, 1, GROUP_W, tm),
                                    lambda i: (0, i // per_seq, 0, i % per_seq))
            out_specs.append(spec)
            out_shape.append(jax.ShapeDtypeStruct((depth, m // seq_len, GROUP_W, seq_len), dtype))
            cols_at.append(k)
    first_stacked = 3
    aliases = {first_stacked + k: cols_at[k] for k in range(len(stacked))}
    return pl.pallas_call(
        functools.partial(_rms_proj_body, plan=tuple((c, kind) for c, kind, _ in plan),
                          layer=layer),
        grid=(m // tm,),
        in_specs=[pl.BlockSpec((tm, d), lambda i: (i, 0)),
                  pl.BlockSpec((1, d), lambda i: (0, 0)),
                  pl.BlockSpec((d, n), lambda i: (0, 0), pipeline_mode=pl.Buffered(1))]
                 + [pl.BlockSpec(memory_space=pl.ANY)] * len(stacked),
        out_specs=out_specs,
        out_shape=out_shape,
        input_output_aliases=aliases,
        compiler_params=_params("parallel"),
        name="rms_proj",
    )(x, g.reshape(1, d), w, *stacked)


def _conv_body(a_ref, b_ref, g_ref, hist_ref, wdw_ref, bdw_ref, lng_ref, lnb_ref, wpw_ref, bpw_ref,
               y_ref, new_ref, shift_ref, s_ref, taps_ref, *, t, rows, nb):
    ext_ref = shift_ref.at[0]
    ti = pl.program_id(1)
    lead = 32 - CONV_HIST
    n_ext = shift_ref.shape[1]
    tile_rows = min(rows, SUBLANES)
    split = (rows // tile_rows, tile_rows, GROUP_W)

    for bi in range(nb):
        seq = slice(bi * t, (bi + 1) * t)

        @pl.when(ti == 0)
        def _():
            ext_ref[...] = jnp.zeros((n_ext, GROUP_W), F32)
            ext_ref[lead:32, :] = hist_ref[bi]

        @pl.when(ti > 0)
        def _():
            ext_ref[lead:32, :] = ext_ref[t + lead:t + 32, :]

        ext_ref[32:32 + t, :] = a_ref[seq, :] * _sigmoid(b_ref[seq, :])
        ext = ext_ref[...]
        for r in range(1, SUBLANES):
            shift_ref[r] = pltpu.roll(ext, n_ext - r, axis=0)
        for j in range(CONV_K):
            taps_ref[j] = jnp.broadcast_to(wdw_ref[j:j + 1, :], (SUBLANES, GROUP_W))

        new_ref[bi] = ext_ref[t + lead:t + 32, :]

        for c in range(t // rows):
            base = c * rows
            acc = jnp.broadcast_to(bdw_ref[...], (rows, GROUP_W)).reshape(split)
            for j in range(CONV_K):
                blocks, r = divmod(lead + j, SUBLANES)
                start = base + blocks * SUBLANES
                acc = acc + (taps_ref[j, 0:tile_rows, :][None]
                             * shift_ref[r, start:start + rows, :].reshape(split))
            acc = acc.reshape(rows, GROUP_W)
            mu = jnp.mean(acc, axis=-1, keepdims=True)
            xc = acc - mu
            var = jnp.mean(xc * xc, axis=-1, keepdims=True)
            yn = xc * lax.rsqrt(var + EPS) * lng_ref[...] + lnb_ref[...]
            s_ref[bi * t + base:bi * t + base + rows, :] = _silu(yn).astype(BF16)

    y = jnp.dot(s_ref[...], wpw_ref[...], preferred_element_type=F32) + bpw_ref[...]
    y_ref[...] = (y * _silu(g_ref[...])).astype(y_ref.dtype)


def conv_mixer(a, b, g, hist, w_dw, b_dw, ln_g, ln_b, w_pw, b_pw, n, l, t, nb, out_dtype):
    w = a.shape[1]
    rows = min(t, 32)
    tiles = l // t
    tile = pl.BlockSpec((nb * t, w), lambda i, j: (i * tiles + j, 0))
    state = pl.BlockSpec((nb, CONV_HIST, w), lambda i, j: (i, 0, 0))
    vec = pl.BlockSpec((1, w), lambda i, j: (0, 0))
    return pl.pallas_call(
        functools.partial(_conv_body, t=t, rows=rows, nb=nb),
        grid=(n // nb, tiles),
        in_specs=[tile, tile, tile, state,
                  pl.BlockSpec((CONV_K, w), lambda i, j: (0, 0)),
                  vec, vec, vec,
                  pl.BlockSpec((w, w), lambda i, j: (0, 0)),
                  vec],
        out_specs=[tile, state],
        out_shape=[jax.ShapeDtypeStruct((n * l, w), out_dtype),
                   jax.ShapeDtypeStruct((n, CONV_HIST, w), F32)],
        scratch_shapes=[pltpu.VMEM((SUBLANES, pl.cdiv(32 + t, SUBLANES) * SUBLANES, w), F32),
                        pltpu.VMEM((nb * t, w), BF16), pltpu.VMEM((CONV_K, SUBLANES, w), F32)],
        compiler_params=_params("parallel", "arbitrary"),
        name="conv_mixer",
    )(a, b, g, hist, w_dw, b_dw.reshape(1, w), ln_g.reshape(1, w), ln_b.reshape(1, w),
      w_pw, b_pw.reshape(1, w))


def _pool_body(u_ref, g_ref, hist_ref, w_ref, scale_ref, y_ref, new_ref, ext_ref, d_ref,
               *, t, pos0, nb):
    ti = pl.program_id(1)
    lead = 16 - POOL_HIST
    pos = pos0 + ti * t + lax.broadcasted_iota(jnp.int32, (t, POOL_GROUP), 0)

    for bi in range(nb):
        seq = slice(bi * t, (bi + 1) * t)

        @pl.when(ti == 0)
        def _():
            ext_ref[lead:16, :] = hist_ref[bi]

        @pl.when(ti > 0)
        def _():
            ext_ref[lead:16, :] = ext_ref[t + lead:t + 16, :]

        ext_ref[16:16 + t, :] = u_ref[seq, :]
        new_ref[bi] = ext_ref[t + lead:t + 16, :]
        for gi, win in enumerate(POOL_WINDOWS):
            sl = slice(gi * POOL_GROUP, (gi + 1) * POOL_GROUP)
            u = u_ref[seq, sl]
            tot = u
            for k in range(1, win):
                tot = tot + ext_ref[16 - k:16 - k + t, sl]
            cnt = jnp.minimum(win, pos + 1).astype(F32)
            d_ref[seq, sl] = (tot / cnt - u).astype(BF16)

    for gi in range(len(POOL_WINDOWS)):
        sl = slice(gi * POOL_GROUP, (gi + 1) * POOL_GROUP)
        y = jnp.dot(d_ref[:, sl], w_ref[gi], preferred_element_type=F32) * scale_ref[:, sl]
        y_ref[:, sl] = (y * _silu(g_ref[:, sl])).astype(y_ref.dtype)


def pool_mixer(u, g, hist, w_pool, scale, n, l, t, nb, pos0, out_dtype):
    w = u.shape[1]
    tiles = l // t
    tile = pl.BlockSpec((nb * t, w), lambda i, j: (i * tiles + j, 0))
    state = pl.BlockSpec((nb, POOL_HIST, w), lambda i, j: (i, 0, 0))
    return pl.pallas_call(
        functools.partial(_pool_body, t=t, pos0=pos0, nb=nb),
        grid=(n // nb, tiles),
        in_specs=[tile, tile, state,
                  pl.BlockSpec(w_pool.shape, lambda i, j: (0, 0, 0)),
                  pl.BlockSpec((1, w), lambda i, j: (0, 0))],
        out_specs=[tile, state],
        out_shape=[jax.ShapeDtypeStruct((n * l, w), out_dtype),
                   jax.ShapeDtypeStruct((n, POOL_HIST, w), F32)],
        scratch_shapes=[pltpu.VMEM((16 + t, w), F32), pltpu.VMEM((nb * t, w), BF16)],
        compiler_params=_params("parallel", "arbitrary"),
        name="pool_mixer",
    )(u, g, hist, w_pool, scale.reshape(1, w))


def _suffix_sum_matrix(blk):
    r = lax.broadcasted_iota(jnp.int32, (blk, blk), 0)
    c = lax.broadcasted_iota(jnp.int32, (blk, blk), 1)
    half = jnp.concatenate([(r >= c).astype(BF16), jnp.ones((blk, blk), BF16)], axis=1)
    return jnp.concatenate([half, half], axis=0)


def _sb_block(nz, tri, carry, mask):
    r = jnp.dot(_sb_log_keep(nz, mask), tri, preferred_element_type=F32)
    return _sb_weights(r, carry, nz, mask)


def _sb_log_keep(nz, mask):
    lg = jnp.minimum(nz, 0.0) - jnp.log(1.0 + jnp.exp2(jnp.abs(nz) * (-LOG2_E)))
    if mask is not None:
        lg = jnp.where(mask, lg, 0.0)
    hi = lg.astype(BF16)
    lo = (lg - hi.astype(F32)).astype(BF16)
    return jnp.concatenate([hi, lo], axis=1)


def _sb_weights(r, carry, nz, mask):
    blk = nz.shape[1]
    a = jnp.exp(r[:, :blk] + carry - nz)
    if mask is not None:
        a = jnp.where(mask, a, 0.0)
    return a, carry + r[:, blk:]


def _sb_body(pt_ref, bias_ref, q_ref, kt_ref, v_ref, g_ref, tri_ref,
             qs_ref, kn_ref, vn_ref, gs_ref, nbias_ref, ckt_hbm, cvt_hbm,
             o_ref, os_ref,
             qq_ref, nz_ref, hl_ref, r_ref, aa_ref, carry_ref, acc_ref,
             kbuf_ref, vbuf_ref, sem_ref, qall_ref, scarry_ref, sacc_ref, pad_ref, ktb_ref, vtb_ref,
             snz_ref, shl_ref, sr_ref, sa_ref, *, ls, pages, n_pages, n_seq, layer):
    tq = SB_PROMPT_BLOCK
    sub = LANES
    halves = tq // sub
    pairs = SB_HEADS // 2
    b = pl.program_id(0)
    i = pl.program_id(1)
    tiles = pl.num_programs(1)
    even_q = lax.broadcasted_iota(jnp.int32, (tq, LANES), 1) < SB_HEAD_DIM
    even_v = lax.broadcasted_iota(jnp.int32, (sub, LANES), 1) < SB_HEAD_DIM
    row = lax.broadcasted_iota(jnp.int32, (2 * tq, sub), 0)
    top = row < tq
    col = lax.broadcasted_iota(jnp.int32, (2 * tq, sub), 1)
    qpos = jnp.where(top, row, row - tq)
    zero = jnp.zeros((), BF16)
    nt = (((1,), (1,)), ((), ()))

    blk = PAGE_SIZE
    rows = ls * SB_HEADS
    w = SB_HEADS * SB_HEAD_DIM
    steps = n_pages // pages
    total = n_seq * steps
    head_mask = (lax.broadcasted_iota(jnp.int32, (SB_HEADS, w), 1) // SB_HEAD_DIM
                 == lax.broadcasted_iota(jnp.int32, (SB_HEADS, w), 0))

    def page_copies(at):
        seq = at // steps
        group = at % steps
        slot = at % SB_SAMPLE_RING
        copies = []
        for r in range(pages):
            page = pt_ref[seq * n_pages + n_pages - 1 - (group * pages + r)]
            copies.append(pltpu.make_async_copy(ckt_hbm.at[layer, page], kbuf_ref.at[slot, r],
                                                sem_ref.at[slot, r]))
            copies.append(pltpu.make_async_copy(cvt_hbm.at[layer, page], vbuf_ref.at[slot, r],
                                                sem_ref.at[slot, pages + r]))
        return copies

    def sample_fetch(at):
        sb = at // steps

        @pl.when(at == 0)
        def _():
            for ahead in range(SB_SAMPLE_RING - 1):
                for copy in page_copies(ahead):
                    copy.start()

        @pl.when(at + (SB_SAMPLE_RING - 1) < total)
        def _():
            for copy in page_copies(at + (SB_SAMPLE_RING - 1)):
                copy.start()

        @pl.when((at < total) & (at % steps == 0))
        def _():
            for t in range(ls):
                qt = qs_ref[pl.ds(sb * ls + t, 1), :] * (-(SB_HEAD_DIM ** -0.5))
                qall_ref[t * SB_HEADS:(t + 1) * SB_HEADS, :] = (
                    jnp.where(head_mask, qt, 0.0).astype(BF16))
            key = lax.broadcasted_iota(jnp.int32, (rows, blk), 1)
            qry = lax.broadcasted_iota(jnp.int32, (rows, blk), 0) // SB_HEADS
            pad_ref[...] = jnp.zeros_like(pad_ref)
            for t in range(ls):
                pad_ref[t:t + 1, :] = kn_ref[pl.ds(sb * ls + t, 1), :]
            nz = lax.dot_general(qall_ref[...], pad_ref[...].astype(BF16), nt,
                                 preferred_element_type=F32)
            for t in range(ls):
                pad_ref[t:t + 1, :] = vn_ref[pl.ds(sb * ls + t, 1), :]
            a, carry = _sb_block(nz + nbias_ref[...], tri_ref[...], jnp.zeros((rows, blk), F32),
                                 key < qry)
            scarry_ref[...] = carry
            sacc_ref[...] = jnp.dot(a.astype(BF16), pad_ref[...].astype(BF16),
                                    preferred_element_type=F32)

        @pl.when(at < total)
        def _():
            for copy in page_copies(at):
                copy.wait()

    def sample_pages(at):
        slot = at % SB_SAMPLE_RING
        tri = tri_ref[...]
        for r in range(pages):
            ktb_ref[:, r * blk:(r + 1) * blk] = kbuf_ref[slot, r].astype(BF16)
            vtb_ref[:, r * blk:(r + 1) * blk] = vbuf_ref[slot, r].astype(BF16)
        snz_ref[...] = (jnp.dot(qall_ref[...], ktb_ref[...], preferred_element_type=F32)
                        + jnp.concatenate([nbias_ref[...]] * pages, axis=1))
        for r in range(pages):
            shl_ref[r * rows:(r + 1) * rows, :] = _sb_log_keep(
                snz_ref[:, r * blk:(r + 1) * blk], None)
        sr_ref[...] = jnp.dot(shl_ref[...], tri, preferred_element_type=F32)
        carry = scarry_ref[...]
        for r in range(pages):
            a, carry = _sb_weights(sr_ref[r * rows:(r + 1) * rows, :], carry,
                                   snz_ref[:, r * blk:(r + 1) * blk], None)
            sa_ref[:, r * blk:(r + 1) * blk] = a.astype(BF16)
        scarry_ref[...] = carry
        sacc_ref[...] += lax.dot_general(sa_ref[...], vtb_ref[...], nt, preferred_element_type=F32)

    def sample_finish(at):
        sb = at // steps

        @pl.when((at < total) & (at % steps == steps - 1))
        def _():
            for t in range(ls):
                o = jnp.where(head_mask, sacc_ref[t * SB_HEADS:(t + 1) * SB_HEADS, :], 0.0)
                o = jnp.sum(o, axis=0, keepdims=True)
                gate = gs_ref[pl.ds(sb * ls + t, 1), :]
                os_ref[pl.ds(sb * ls + t, 1), :] = (o * _silu(gate)).astype(os_ref.dtype)

    for p in range(pairs):
        qp = q_ref[:, p * LANES:(p + 1) * LANES] * (-(SB_HEAD_DIM ** -0.5))
        qq_ref[p] = jnp.concatenate([jnp.where(even_q, qp, 0.0), jnp.where(even_q, 0.0, qp)],
                                    axis=0).astype(BF16)
    carry_ref[...] = jnp.zeros_like(carry_ref)
    acc_ref[...] = jnp.zeros_like(acc_ref)

    def key_block(j, diagonal):
        start = pl.multiple_of(j * tq, tq)
        masks = [col + h * sub < qpos if diagonal else None for h in range(halves)]
        for p in range(pairs):
            kt = kt_ref[0, 0, p * LANES:(p + 1) * LANES, pl.ds(start, tq)].astype(BF16)
            nbias = jnp.where(top, -bias_ref[2 * p], -bias_ref[2 * p + 1])
            nz = jnp.dot(qq_ref[p], kt, preferred_element_type=F32)
            for h in range(halves):
                nz_ref[p, :, h * sub:(h + 1) * sub] = nz[:, h * sub:(h + 1) * sub] + nbias
        for p in range(pairs):
            for h in range(halves):
                hl_ref[p, h] = _sb_log_keep(nz_ref[p, :, h * sub:(h + 1) * sub], masks[h])
        for p in range(pairs):
            for h in range(halves):
                r_ref[p, h] = jnp.dot(hl_ref[p, h], tri_ref[...], preferred_element_type=F32)
        for p in range(pairs):
            carry = carry_ref[p]
            for h in reversed(range(halves)):
                a, carry = _sb_weights(r_ref[p, h], carry, nz_ref[p, :, h * sub:(h + 1) * sub],
                                       masks[h])
                ab = a.astype(BF16)
                aa_ref[p, :, 2 * h * sub:(2 * h + 1) * sub] = ab[:tq]
                aa_ref[p, :, (2 * h + 1) * sub:(2 * h + 2) * sub] = ab[tq:]
            carry_ref[p] = carry
        for p in range(pairs):
            vb = v_ref[pl.ds(start, tq), p * LANES:(p + 1) * LANES]
            vv = []
            for h in range(halves):
                vh = vb[h * sub:(h + 1) * sub]
                vv += [jnp.where(even_v, vh, zero), jnp.where(even_v, zero, vh)]
            acc_ref[p] += jnp.dot(aa_ref[p], jnp.concatenate(vv, axis=0),
                                  preferred_element_type=F32)

    first = b * (tiles * (tiles + 1) // 2) + i * (i + 1) // 2
    sample_fetch(first)
    key_block(i, True)
    sample_pages(first)
    sample_finish(first)

    def step(it, _):
        at = first + 1 + it
        sample_fetch(at)
        key_block(i - 1 - it, False)
        sample_pages(at)
        sample_finish(at)
        return 0

    lax.fori_loop(0, i, step, 0)
    for p in range(pairs):
        sl = slice(p * LANES, (p + 1) * LANES)
        o_ref[:, sl] = (acc_ref[p] * _silu(g_ref[:, sl])).astype(o_ref.dtype)

    @pl.when((b == pl.num_programs(0) - 1) & (i == tiles - 1))
    def _():
        def rest(at, _):
            sample_fetch(at)
            sample_pages(at)
            sample_finish(at)
            return 0

        lax.fori_loop(pl.num_programs(0) * (tiles * (tiles + 1) // 2), total, rest, 0)


def sb_attend(q, kt, v, g, qs, ks_new, vs_new, gs, bias, cache_kt, cache_vt, page_table, layer, ls):
    _, n, w, l = kt.shape
    m_s = qs.shape[0]
    n_seq, n_pages = page_table.shape
    assert n_seq * (n_pages // SB_SAMPLE_PAGES) >= SB_SAMPLE_RING
    tq = SB_PROMPT_BLOCK
    sub = LANES
    halves = tq // sub
    pairs = SB_HEADS // 2
    tiles = l // tq
    pages = SB_SAMPLE_PAGES
    rows = ls * SB_HEADS
    blk = PAGE_SIZE
    tile = pl.BlockSpec((tq, w), lambda b, i, pt: (b * tiles + i, 0))
    whole = pl.BlockSpec((m_s, w), lambda b, i, pt: (0, 0))
    nbias_rows = jnp.broadcast_to(-jnp.tile(bias, ls)[:, None], (rows, blk)).astype(F32)
    grid_spec = pltpu.PrefetchScalarGridSpec(
        num_scalar_prefetch=1,
        grid=(n, tiles),
        in_specs=[pl.BlockSpec(memory_space=pltpu.SMEM), tile,
                  pl.BlockSpec((1, 1, w, l), lambda b, i, pt: (layer, b, 0, 0)),
                  pl.BlockSpec((l, w), lambda b, i, pt: (b, 0)), tile,
                  pl.BlockSpec((2 * sub, 2 * sub), lambda b, i, pt: (0, 0)),
                  whole, whole, whole, whole,
                  pl.BlockSpec((rows, blk), lambda b, i, pt: (0, 0)),
                  pl.BlockSpec(memory_space=pl.ANY), pl.BlockSpec(memory_space=pl.ANY)],
        out_specs=[tile, whole],
        scratch_shapes=[pltpu.VMEM((pairs, 2 * tq, LANES), BF16),
                        pltpu.VMEM((pairs, 2 * tq, tq), F32),
                        pltpu.VMEM((pairs, halves, 2 * tq, 2 * sub), BF16),
                        pltpu.VMEM((pairs, halves, 2 * tq, 2 * sub), F32),
                        pltpu.VMEM((pairs, tq, halves * 2 * sub), BF16),
                        pltpu.VMEM((pairs, 2 * tq, sub), F32),
                        pltpu.VMEM((pairs, tq, LANES), F32),
                        pltpu.VMEM((SB_SAMPLE_RING, pages, w, blk), F32),
                        pltpu.VMEM((SB_SAMPLE_RING, pages, w, blk), F32),
                        pltpu.SemaphoreType.DMA((SB_SAMPLE_RING, 2 * pages)),
                        pltpu.VMEM((rows, w), BF16), pltpu.VMEM((rows, blk), F32),
                        pltpu.VMEM((rows, w), F32), pltpu.VMEM((blk, w), F32),
                        pltpu.VMEM((w, pages * blk), BF16), pltpu.VMEM((w, pages * blk), BF16),
                        pltpu.VMEM((rows, pages * blk), F32),
                        pltpu.VMEM((pages * rows, 2 * blk), BF16),
                        pltpu.VMEM((pages * rows, 2 * blk), F32),
                        pltpu.VMEM((rows, pages * blk), BF16)],
    )
    return pl.pallas_call(
        functools.partial(_sb_body, ls=ls, pages=pages, n_pages=n_pages, n_seq=n_seq, layer=layer),
        grid_spec=grid_spec,
        out_shape=[jax.ShapeDtypeStruct((n * l, w), BF16), jax.ShapeDtypeStruct((m_s, w), F32)],
        compiler_params=_params("arbitrary", "arbitrary"),
        name="sb_attend",
    )(page_table.reshape(-1), bias, q, kt, v, g, _suffix_sum_matrix(sub),
      qs, ks_new, vs_new, gs, nbias_rows, cache_kt, cache_vt)


def _softmax_rows(s):
    e = jnp.exp(s - jnp.max(s, axis=-1, keepdims=True))
    return e / jnp.sum(e, axis=-1, keepdims=True)


def _mem_body(q_ref, g_ref, mk_ref, mv_ref, o_ref, *, t, nb):
    nt = (((1,), (1,)), ((), ()))
    for bi in range(nb):
        seq = slice(bi * t, (bi + 1) * t)
        for h in range(MEM_HEADS):
            sl = slice(h * MEM_HEAD_DIM, (h + 1) * MEM_HEAD_DIM)
            s = lax.dot_general(q_ref[seq, sl].astype(BF16), mk_ref[bi, :, sl].astype(BF16), nt,
                                preferred_element_type=F32) * (MEM_HEAD_DIM ** -0.5)
            o = jnp.dot(_softmax_rows(s).astype(BF16), mv_ref[bi, :, sl].astype(BF16),
                        preferred_element_type=F32)
            o_ref[seq, sl] = (o * _silu(g_ref[seq, sl])).astype(o_ref.dtype)


def _mem_paired_body(q_ref, g_ref, mk_ref, mv_ref, o_ref, *, t, nb):
    nt = (((1,), (1,)), ((), ()))
    n_rows = mk_ref.shape[2] * mk_ref.shape[3]
    row_head = lax.broadcasted_iota(jnp.int32, (MEM_HEADS * t, n_rows), 1) % MEM_HEADS
    q_head = lax.broadcasted_iota(jnp.int32, (MEM_HEADS * t, n_rows), 0) // t
    for bi in range(nb):
        seq = slice(bi * t, (bi + 1) * t)
        heads = [slice(h * MEM_HEAD_DIM, (h + 1) * MEM_HEAD_DIM) for h in range(MEM_HEADS)]
        q = jnp.concatenate([q_ref[seq, sl] for sl in heads], axis=0).astype(BF16)
        mk = mk_ref[0, bi].reshape(n_rows, MEM_HEAD_DIM).astype(BF16)
        mv = mv_ref[0, bi].reshape(n_rows, MEM_HEAD_DIM).astype(BF16)
        s = lax.dot_general(q, mk, nt, preferred_element_type=F32) * (MEM_HEAD_DIM ** -0.5)
        p = _softmax_rows(jnp.where(row_head == q_head, s, -jnp.inf))
        o = jnp.dot(p.astype(BF16), mv, preferred_element_type=F32)
        for h, sl in enumerate(heads):
            o_ref[seq, sl] = (o[h * t:(h + 1) * t] * _silu(g_ref[seq, sl])).astype(o_ref.dtype)


def mem_attend(q, g, mk, mv, n, l, t, nb, out_dtype, layer=None):
    w = q.shape[1]
    tiles = l // t
    tile = pl.BlockSpec((nb * t, w), lambda b, i: (b * tiles + i, 0))
    if layer is None:
        mem = pl.BlockSpec((nb,) + mk.shape[1:], lambda b, i: (b, 0, 0))
    else:
        mem = pl.BlockSpec((1, nb) + mk.shape[2:], lambda b, i: (layer, b, 0, 0, 0))
    return pl.pallas_call(
        functools.partial(_mem_body if layer is None else _mem_paired_body, t=t, nb=nb),
        grid=(n // nb, tiles),
        in_specs=[tile, tile, mem, mem],
        out_specs=tile,
        out_shape=jax.ShapeDtypeStruct((n * l, w), out_dtype),
        compiler_params=_params("parallel", "arbitrary"),
        name="mem_attend",
    )(q, g, mk, mv)


def _out_proj_body(c_ref, p_ref, s_ref, m_ref, w_ref, pg_ref, x_ref, fg_ref, o_ref, *, final):
    acc = None
    for gi, part in enumerate((c_ref, p_ref, s_ref, m_ref)):
        d = jnp.dot(part[...].astype(BF16), w_ref[gi * GROUP_W:(gi + 1) * GROUP_W, :],
                    preferred_element_type=F32)
        acc = d if acc is None else acc + d
    y = x_ref[...] + _rms(acc, pg_ref[...])
    if final:
        y = _rms(y, fg_ref[...])
    o_ref[...] = y


def out_proj(parts, w, post_g, x, final_g, tm, final):
    m, d = x.shape
    part = pl.BlockSpec((tm, GROUP_W), lambda i: (i, 0))
    vec = pl.BlockSpec((1, d), lambda i: (0, 0))
    row = pl.BlockSpec((tm, d), lambda i: (i, 0))
    return pl.pallas_call(
        functools.partial(_out_proj_body, final=final),
        grid=(m // tm,),
        in_specs=[part, part, part, part, pl.BlockSpec(w.shape, lambda i: (0, 0)), vec, row, vec],
        out_specs=row,
        out_shape=jax.ShapeDtypeStruct((m, d), F32),
        compiler_params=_params("parallel"),
        name="out_proj",
    )(*parts, w, post_g.reshape(1, d), x, final_g.reshape(1, d))


def _project(x, l, lw, tm, keys_on_lanes, layer=0, depth=1, stacked=()):
    plan = [(c, "rows", F32) for c in range(len(MIX_GROUPS))]
    if keys_on_lanes:
        for c, name in enumerate(MIX_GROUPS):
            if name in BF16_GROUPS:
                plan[c] = (c, "rows", BF16)
        plan[K_GROUP] = (K_GROUP, "cols", F32)
        plan[V_GROUP:V_GROUP + 1] = [(V_GROUP, "cols", F32), (V_GROUP, "rows", BF16)]
    z = list(rms_proj(x, lw[0], lw[2], tm, tuple(plan), l, layer, depth, stacked))
    v_rows = z.pop(V_GROUP + 1) if keys_on_lanes else z[V_GROUP]
    return z, v_rows


def _mix(x, z, y_s, n, l, conv_hist, pool_hist, pos0, mem_fn, lw, final_g, final, tiles):
    (_, post_g, _, w_dw, b_dw, ln_g, ln_b, w_pw, b_pw, pool_w, pool_scale, _, w_out) = lw
    _, tm_out, t_local, nb, mix_dtype = tiles
    ca, cb, cg, pu, pg, _, _, _, _, mq, mg = z
    y_c, conv_new = conv_mixer(ca, cb, cg, conv_hist, w_dw, b_dw, ln_g, ln_b, w_pw, b_pw, n, l,
                               t_local, nb, mix_dtype)
    y_p, pool_new = pool_mixer(pu, pg, pool_hist, pool_w, pool_scale, n, l, t_local, nb, pos0,
                               mix_dtype)
    y_m = mem_fn(mq, mg, t_local, nb, mix_dtype)
    x = out_proj([y_c, y_p, y_s, y_m], w_out, post_g, x, final_g, tm_out, final)
    return x, conv_new, pool_new


def kernel(x_prompt, x_sample, mem_prompt, cache_k, cache_v, cache_mem_k, cache_mem_v, state_conv,
           state_pool, page_table, pre_g, post_g, w_in, conv_w_dw, conv_b_dw, conv_ln_g, conv_ln_b,
           conv_w_pw, conv_b_pw, pool_w, pool_scale, sb_bias, mem_g, w_mem_kv, w_out, final_g):
    bp, lp, d = x_prompt.shape
    bs, ls, _ = x_sample.shape
    depth = w_in.shape[0]
    n_mem = mem_prompt.shape[1]
    n_phys = cache_k.shape[1]
    past = page_table.shape[1] * PAGE_SIZE
    w = GROUP_W

    xp = x_prompt.reshape(bp * lp, d)
    xs = x_sample.reshape(bs * ls, d)
    mem = mem_prompt.reshape(bp * n_mem, d)
    ckt = jnp.transpose(cache_k, (0, 1, 3, 4, 2)).reshape(depth, n_phys, w, PAGE_SIZE)
    cvt = jnp.transpose(cache_v, (0, 1, 3, 4, 2)).reshape(depth, n_phys, w, PAGE_SIZE)
    pair_shape = (depth, bs, n_mem // 2, 2 * MEM_HEADS, MEM_HEAD_DIM)
    cmk = cache_mem_k.reshape(pair_shape)
    cmv = cache_mem_v.reshape(pair_shape)
    zero_conv = jnp.zeros((bp, CONV_HIST, w), F32)
    zero_pool = jnp.zeros((bp, POOL_HIST, w), F32)
    prompt_tiles = (512, 512, 512, 1, BF16)
    sample_tiles = (bs * ls, bs * ls, ls, 8, F32)

    outs = [[] for _ in range(8)]
    kv_p = ()
    for l in range(depth):
        final = l == depth - 1
        lw = (pre_g[l], post_g[l], w_in[l].astype(BF16), conv_w_dw[l], conv_b_dw[l], conv_ln_g[l],
              conv_ln_b[l], conv_w_pw[l].astype(BF16), conv_b_pw[l], pool_w[l].astype(BF16),
              pool_scale[l], sb_bias[l], w_out[l].astype(BF16))
        mk_p, mv_p = [a.reshape(bp, n_mem, w)
                      for a in rms_proj(mem, mem_g[l], w_mem_kv[l].astype(BF16), 256)]

        def mem_prompt_fn(q, g, t, nb, dtype, mk_p=mk_p, mv_p=mv_p):
            return mem_attend(q, g, mk_p, mv_p, bp, lp, t, nb, dtype)

        def mem_sample_fn(q, g, t, nb, dtype, l=l):
            return mem_attend(q, g, cmk, cmv, bs, ls, t, nb, dtype, layer=l)

        zp, vp_rows = _project(xp, lp, lw, prompt_tiles[0], True, l, depth, kv_p)
        zs, _ = _project(xs, ls, lw, sample_tiles[0], False)
        q_at, k_at, v_at, g_at = (MIX_GROUPS.index(name) for name in ("sb_q", "sb_k", "sb_v", "sb_gate"))
        k_p, v_p, k_s, v_s = zp[k_at], zp[v_at], zs[k_at], zs[v_at]
        kv_p = (k_p, v_p)
        ys_p, ys_s = sb_attend(zp[q_at], k_p, vp_rows, zp[g_at], zs[q_at], k_s, v_s, zs[g_at],
                               sb_bias[l], ckt, cvt, page_table, l, ls)
        xp, c_p, p_p = _mix(xp, zp, ys_p, bp, lp, zero_conv, zero_pool, 0, mem_prompt_fn, lw,
                            final_g, final, prompt_tiles)
        xs, c_s, p_s = _mix(xs, zs, ys_s, bs, ls, state_conv[l], state_pool[l], past,
                            mem_sample_fn, lw, final_g, final, sample_tiles)
        for lst, a in zip(outs, (k_s, v_s, c_p, c_s, p_p, p_s, mk_p, mv_p)):
            lst.append(a)

    k_s, v_s, c_p, c_s, p_p, p_s, mk_p, mv_p = [jnp.stack(a) for a in outs]
    k_p, v_p = kv_p
    heads_first = (depth, bp, SB_HEADS, SB_HEAD_DIM, lp)
    return (xp.reshape(bp, lp, d), xs.reshape(bs, ls, d),
            jnp.transpose(k_p.reshape(heads_first), (0, 1, 4, 2, 3)),
            jnp.transpose(v_p.reshape(heads_first), (0, 1, 4, 2, 3)),
            k_s.reshape(depth, bs, ls, SB_HEADS, SB_HEAD_DIM),
            v_s.reshape(depth, bs, ls, SB_HEADS, SB_HEAD_DIM),
            c_p, c_s, p_p, p_s,
            mk_p.reshape(depth, bp, n_mem, MEM_HEADS, MEM_HEAD_DIM),
            mv_p.reshape(depth, bp, n_mem, MEM_HEADS, MEM_HEAD_DIM))
```

```python
import functools

import jax
import jax.numpy as jnp
from jax import lax
from jax.experimental import pallas as pl
from jax.experimental.pallas import tpu as pltpu

F32 = jnp.float32
BF16 = jnp.bfloat16

EPS = 1e-6
GROUP_W = 512
MIX_GROUPS = ("conv_a", "conv_glu_gate", "conv_silu_gate", "pool_u", "pool_gate", "sb_q", "sb_k",
              "sb_v", "sb_gate", "mem_q", "mem_gate")
BF16_GROUPS = ("sb_q", "mem_q")
K_GROUP = MIX_GROUPS.index("sb_k")
V_GROUP = MIX_GROUPS.index("sb_v")
CONV_K = 31
CONV_HIST = CONV_K - 1
POOL_WINDOWS = (2, 4, 8, 16)
POOL_GROUP = 128
POOL_HIST = max(POOL_WINDOWS) - 1
SB_HEADS = 8
SB_HEAD_DIM = 64
MEM_HEADS = 4
MEM_HEAD_DIM = 128
PAGE_SIZE = 128
LANES = 128
SUBLANES = 8
SB_PROMPT_BLOCK = 256
SB_SAMPLE_PAGES = 8
SB_SAMPLE_RING = 3
LOG2_E = 1.4426950408889634
VMEM_LIMIT_BYTES = 56 * 1024 * 1024


def _params(*semantics):
    return pltpu.CompilerParams(dimension_semantics=semantics, vmem_limit_bytes=VMEM_LIMIT_BYTES)


def _sigmoid(x):
    return 1.0 / (1.0 + jnp.exp(-x))


def _silu(x):
    return x * _sigmoid(x)


def _rms(x, g):
    return x * lax.rsqrt(jnp.mean(x * x, axis=-1, keepdims=True) + EPS) * g


def _rms_proj_body(x_ref, g_ref, w_ref, *rest, plan, layer):
    o_refs = rest[len(rest) - len(plan):]
    h = _rms(x_ref[...], g_ref[...]).astype(BF16)
    done = {}
    for (c, kind), o_ref in zip(plan, o_refs):
        if c not in done:
            done = {c: jnp.dot(h, w_ref[:, c * GROUP_W:(c + 1) * GROUP_W],
                               preferred_element_type=F32)}
        y = done[c]
        if kind == "rows":
            o_ref[...] = y.astype(o_ref.dtype)
        elif o_ref.shape[0] == 1:
            o_ref[0, 0] = y.T
        else:
            o_ref[...] = jnp.zeros(o_ref.shape, o_ref.dtype)
            o_ref[layer, 0] = y.T


def rms_proj(x, g, w, tm, plan=None, seq_len=None, layer=0, depth=1, stacked=()):
    m, d = x.shape
    n = w.shape[1]
    if plan is None:
        plan = tuple((c, "rows", F32) for c in range(n // GROUP_W))
    out_specs, out_shape, cols_at = [], [], []
    for k, (_, kind, dtype) in enumerate(plan):
        if kind == "rows":
            out_specs.append(pl.BlockSpec((tm, GROUP_W), lambda i: (i, 0)))
            out_shape.append(jax.ShapeDtypeStruct((m, GROUP_W), dtype))
        else:
            per_seq = seq_len // tm
            if stacked:
                spec = pl.BlockSpec((1, 1, GROUP_W, tm),
                                    lambda i: (layer, i // per_seq, 0, i % per_seq))
            else:
                spec = pl.BlockSpec((depth, 1, GROUP_W, tm),
                                    lambda i: (0, i // per_seq, 0, i % per_seq))
            out_specs.append(spec)
            out_shape.append(jax.ShapeDtypeStruct((depth, m // seq_len, GROUP_W, seq_len), dtype))
            cols_at.append(k)
    first_stacked = 3
    aliases = {first_stacked + k: cols_at[k] for k in range(len(stacked))}
    return pl.pallas_call(
        functools.partial(_rms_proj_body, plan=tuple((c, kind) for c, kind, _ in plan),
                          layer=layer),
        grid=(m // tm,),
        in_specs=[pl.BlockSpec((tm, d), lambda i: (i, 0)),
                  pl.BlockSpec((1, d), lambda i: (0, 0)),
                  pl.BlockSpec((d, n), lambda i: (0, 0), pipeline_mode=pl.Buffered(1))]
                 + [pl.BlockSpec(memory_space=pl.ANY)] * len(stacked),
        out_specs=out_specs,
        out_shape=out_shape,
        input_output_aliases=aliases,
        compiler_params=_params("parallel"),
        name="rms_proj",
    )(x, g.reshape(1, d), w, *stacked)


def _conv_body(a_ref, b_ref, g_ref, hist_ref, wdw_ref, bdw_ref, lng_ref, lnb_ref, wpw_ref, bpw_ref,
               y_ref, new_ref, shift_ref, s_ref, taps_ref, *, t, rows, nb):
    ext_ref = shift_ref.at[0]
    ti = pl.program_id(1)
    lead = 32 - CONV_HIST
    n_ext = shift_ref.shape[1]
    tile_rows = min(rows, SUBLANES)
    split = (rows // tile_rows, tile_rows, GROUP_W)

    for bi in range(nb):
        seq = slice(bi * t, (bi + 1) * t)

        @pl.when(ti == 0)
        def _():
            ext_ref[...] = jnp.zeros((n_ext, GROUP_W), F32)
            ext_ref[lead:32, :] = hist_ref[bi]

        @pl.when(ti > 0)
        def _():
            ext_ref[lead:32, :] = ext_ref[t + lead:t + 32, :]

        ext_ref[32:32 + t, :] = a_ref[seq, :] * _sigmoid(b_ref[seq, :])
        ext = ext_ref[...]
        for r in range(1, SUBLANES):
            shift_ref[r] = pltpu.roll(ext, n_ext - r, axis=0)
        for j in range(CONV_K):
            taps_ref[j] = jnp.broadcast_to(wdw_ref[j:j + 1, :], (SUBLANES, GROUP_W))

        new_ref[bi] = ext_ref[t + lead:t + 32, :]

        for c in range(t // rows):
            base = c * rows
            acc = jnp.broadcast_to(bdw_ref[...], (rows, GROUP_W)).reshape(split)
            for j in range(CONV_K):
                blocks, r = divmod(lead + j, SUBLANES)
                start = base + blocks * SUBLANES
                acc = acc + (taps_ref[j, 0:tile_rows, :][None]
                             * shift_ref[r, start:start + rows, :].reshape(split))
            acc = acc.reshape(rows, GROUP_W)
            mu = jnp.mean(acc, axis=-1, keepdims=True)
            xc = acc - mu
            var = jnp.mean(xc * xc, axis=-1, keepdims=True)
            yn = xc * lax.rsqrt(var + EPS) * lng_ref[...] + lnb_ref[...]
            s_ref[bi * t + base:bi * t + base + rows, :] = _silu(yn).astype(BF16)

    y = jnp.dot(s_ref[...], wpw_ref[...], preferred_element_type=F32) + bpw_ref[...]
    y_ref[...] = (y * _silu(g_ref[...])).astype(y_ref.dtype)


def conv_mixer(a, b, g, hist, w_dw, b_dw, ln_g, ln_b, w_pw, b_pw, n, l, t, nb, out_dtype):
    w = a.shape[1]
    rows = min(t, 32)
    tiles = l // t
    tile = pl.BlockSpec((nb * t, w), lambda i, j: (i * tiles + j, 0))
    state = pl.BlockSpec((nb, CONV_HIST, w), lambda i, j: (i, 0, 0))
    vec = pl.BlockSpec((1, w), lambda i, j: (0, 0))
    return pl.pallas_call(
        functools.partial(_conv_body, t=t, rows=rows, nb=nb),
        grid=(n // nb, tiles),
        in_specs=[tile, tile, tile, state,
                  pl.BlockSpec((CONV_K, w), lambda i, j: (0, 0)),
                  vec, vec, vec,
                  pl.BlockSpec((w, w), lambda i, j: (0, 0)),
                  vec],
        out_specs=[tile, state],
        out_shape=[jax.ShapeDtypeStruct((n * l, w), out_dtype),
                   jax.ShapeDtypeStruct((n, CONV_HIST, w), F32)],
        scratch_shapes=[pltpu.VMEM((SUBLANES, pl.cdiv(32 + t, SUBLANES) * SUBLANES, w), F32),
                        pltpu.VMEM((nb * t, w), BF16), pltpu.VMEM((CONV_K, SUBLANES, w), F32)],
        compiler_params=_params("parallel", "arbitrary"),
        name="conv_mixer",
    )(a, b, g, hist, w_dw, b_dw.reshape(1, w), ln_g.reshape(1, w), ln_b.reshape(1, w),
      w_pw, b_pw.reshape(1, w))


def _pool_body(u_ref, g_ref, hist_ref, w_ref, scale_ref, y_ref, new_ref, ext_ref, d_ref,
               *, t, pos0, nb):
    ti = pl.program_id(1)
    lead = 16 - POOL_HIST
    pos = pos0 + ti * t + lax.broadcasted_iota(jnp.int32, (t, POOL_GROUP), 0)

    for bi in range(nb):
        seq = slice(bi * t, (bi + 1) * t)

        @pl.when(ti == 0)
        def _():
            ext_ref[lead:16, :] = hist_ref[bi]

        @pl.when(ti > 0)
        def _():
            ext_ref[lead:16, :] = ext_ref[t + lead:t + 16, :]

        ext_ref[16:16 + t, :] = u_ref[seq, :]
        new_ref[bi] = ext_ref[t + lead:t + 16, :]
        for gi, win in enumerate(POOL_WINDOWS):
            sl = slice(gi * POOL_GROUP, (gi + 1) * POOL_GROUP)
            u = u_ref[seq, sl]
            tot = u
            for k in range(1, win):
                tot = tot + ext_ref[16 - k:16 - k + t, sl]
            cnt = jnp.minimum(win, pos + 1).astype(F32)
            d_ref[seq, sl] = (tot / cnt - u).astype(BF16)

    for gi in range(len(POOL_WINDOWS)):
        sl = slice(gi * POOL_GROUP, (gi + 1) * POOL_GROUP)
        y = jnp.dot(d_ref[:, sl], w_ref[gi], preferred_element_type=F32) * scale_ref[:, sl]
        y_ref[:, sl] = (y * _silu(g_ref[:, sl])).astype(y_ref.dtype)


def pool_mixer(u, g, hist, w_pool, scale, n, l, t, nb, pos0, out_dtype):
    w = u.shape[1]
    tiles = l // t
    tile = pl.BlockSpec((nb * t, w), lambda i, j: (i * tiles + j, 0))
    state = pl.BlockSpec((nb, POOL_HIST, w), lambda i, j: (i, 0, 0))
    return pl.pallas_call(
        functools.partial(_pool_body, t=t, pos0=pos0, nb=nb),
        grid=(n // nb, tiles),
        in_specs=[tile, tile, state,
                  pl.BlockSpec(w_pool.shape, lambda i, j: (0, 0, 0)),
                  pl.BlockSpec((1, w), lambda i, j: (0, 0))],
        out_specs=[tile, state],
        out_shape=[jax.ShapeDtypeStruct((n * l, w), out_dtype),
                   jax.ShapeDtypeStruct((n, POOL_HIST, w), F32)],
        scratch_shapes=[pltpu.VMEM((16 + t, w), F32), pltpu.VMEM((nb * t, w), BF16)],
        compiler_params=_params("parallel", "arbitrary"),
        name="pool_mixer",
    )(u, g, hist, w_pool, scale.reshape(1, w))


def _suffix_sum_matrix(blk):
    r = lax.broadcasted_iota(jnp.int32, (blk, blk), 0)
    c = lax.broadcasted_iota(jnp.int32, (blk, blk), 1)
    half = jnp.concatenate([(r >= c).astype(BF16), jnp.ones((blk, blk), BF16)], axis=1)
    return jnp.concatenate([half, half], axis=0)


def _sb_block(nz, tri, carry, mask):
    r = jnp.dot(_sb_log_keep(nz, mask), tri, preferred_element_type=F32)
    return _sb_weights(r, carry, nz, mask)


def _sb_log_keep(nz, mask):
    lg = jnp.minimum(nz, 0.0) - jnp.log(1.0 + jnp.exp2(jnp.abs(nz) * (-LOG2_E)))
    if mask is not None:
        lg = jnp.where(mask, lg, 0.0)
    hi = lg.astype(BF16)
    lo = (lg - hi.astype(F32)).astype(BF16)
    return jnp.concatenate([hi, lo], axis=1)


def _sb_weights(r, carry, nz, mask):
    blk = nz.shape[1]
    a = jnp.exp(r[:, :blk] + carry - nz)
    if mask is not None:
        a = jnp.where(mask, a, 0.0)
    return a, carry + r[:, blk:]


def _sb_body(pt_ref, bias_ref, q_ref, kt_ref, v_ref, g_ref, tri_ref,
             qs_ref, kn_ref, vn_ref, gs_ref, nbias_ref, ckt_hbm, cvt_hbm,
             o_ref, os_ref,
             qq_ref, nz_ref, hl_ref, r_ref, aa_ref, carry_ref, acc_ref,
             kbuf_ref, vbuf_ref, sem_ref, qall_ref, scarry_ref, sacc_ref, pad_ref, ktb_ref, vtb_ref,
             snz_ref, shl_ref, sr_ref, sa_ref, *, ls, pages, n_pages, n_seq, layer):
    tq = SB_PROMPT_BLOCK
    sub = LANES
    halves = tq // sub
    pairs = SB_HEADS // 2
    b = pl.program_id(0)
    i = pl.program_id(1)
    tiles = pl.num_programs(1)
    even_q = lax.broadcasted_iota(jnp.int32, (tq, LANES), 1) < SB_HEAD_DIM
    even_v = lax.broadcasted_iota(jnp.int32, (sub, LANES), 1) < SB_HEAD_DIM
    row = lax.broadcasted_iota(jnp.int32, (2 * tq, sub), 0)
    top = row < tq
    zero = jnp.zeros((), BF16)
    nt = (((1,), (1,)), ((), ()))

    blk = PAGE_SIZE
    rows = ls * SB_HEADS
    w = SB_HEADS * SB_HEAD_DIM
    steps = n_pages // pages
    total = n_seq * steps
    head_mask = (lax.broadcasted_iota(jnp.int32, (SB_HEADS, w), 1) // SB_HEAD_DIM
                 == lax.broadcasted_iota(jnp.int32, (SB_HEADS, w), 0))

    def page_copies(at):
        seq = at // steps
        group = at % steps
        slot = at % SB_SAMPLE_RING
        copies = []
        for r in range(pages):
            page = pt_ref[seq * n_pages + n_pages - 1 - (group * pages + r)]
            copies.append(pltpu.make_async_copy(ckt_hbm.at[layer, page], kbuf_ref.at[slot, r],
                                                sem_ref.at[slot, r]))
            copies.append(pltpu.make_async_copy(cvt_hbm.at[layer, page], vbuf_ref.at[slot, r],
                                                sem_ref.at[slot, pages + r]))
        return copies

    def sample_fetch(at):
        sb = at // steps

        @pl.when(at == 0)
        def _():
            for ahead in range(SB_SAMPLE_RING - 1):
                for copy in page_copies(ahead):
                    copy.start()

        @pl.when(at + (SB_SAMPLE_RING - 1) < total)
        def _():
            for copy in page_copies(at + (SB_SAMPLE_RING - 1)):
                copy.start()

        @pl.when((at < total) & (at % steps == 0))
        def _():
            for t in range(ls):
                qt = qs_ref[pl.ds(sb * ls + t, 1), :] * (-(SB_HEAD_DIM ** -0.5))
                qall_ref[t * SB_HEADS:(t + 1) * SB_HEADS, :] = (
                    jnp.where(head_mask, qt, 0.0).astype(BF16))
            key = lax.broadcasted_iota(jnp.int32, (rows, blk), 1)
            qry = lax.broadcasted_iota(jnp.int32, (rows, blk), 0) // SB_HEADS
            pad_ref[...] = jnp.zeros_like(pad_ref)
            for t in range(ls):
                pad_ref[t:t + 1, :] = kn_ref[pl.ds(sb * ls + t, 1), :]
            nz = lax.dot_general(qall_ref[...], pad_ref[...].astype(BF16), nt,
                                 preferred_element_type=F32)
            for t in range(ls):
                pad_ref[t:t + 1, :] = vn_ref[pl.ds(sb * ls + t, 1), :]
            a, carry = _sb_block(nz + nbias_ref[...], tri_ref[...], jnp.zeros((rows, blk), F32),
                                 key < qry)
            scarry_ref[...] = carry
            sacc_ref[...] = jnp.dot(a.astype(BF16), pad_ref[...].astype(BF16),
                                    preferred_element_type=F32)

        @pl.when(at < total)
        def _():
            for copy in page_copies(at):
                copy.wait()

    def sample_pages(at):
        slot = at % SB_SAMPLE_RING
        tri = tri_ref[...]
        for r in range(pages):
            ktb_ref[:, r * blk:(r + 1) * blk] = kbuf_ref[slot, r].astype(BF16)
            vtb_ref[:, r * blk:(r + 1) * blk] = vbuf_ref[slot, r].astype(BF16)
        snz_ref[...] = (jnp.dot(qall_ref[...], ktb_ref[...], preferred_element_type=F32)
                        + jnp.concatenate([nbias_ref[...]] * pages, axis=1))
        for r in range(pages):
            shl_ref[r * rows:(r + 1) * rows, :] = _sb_log_keep(
                snz_ref[:, r * blk:(r + 1) * blk], None)
        sr_ref[...] = jnp.dot(shl_ref[...], tri, preferred_element_type=F32)
        carry = scarry_ref[...]
        for r in range(pages):
            a, carry = _sb_weights(sr_ref[r * rows:(r + 1) * rows, :], carry,
                                   snz_ref[:, r * blk:(r + 1) * blk], None)
            sa_ref[:, r * blk:(r + 1) * blk] = a.astype(BF16)
        scarry_ref[...] = carry
        sacc_ref[...] += lax.dot_general(sa_ref[...], vtb_ref[...], nt, preferred_element_type=F32)

    def sample_finish(at):
        sb = at // steps

        @pl.when((at < total) & (at % steps == steps - 1))
        def _():
            for t in range(ls):
                o = jnp.where(head_mask, sacc_ref[t * SB_HEADS:(t + 1) * SB_HEADS, :], 0.0)
                o = jnp.sum(o, axis=0, keepdims=True)
                gate = gs_ref[pl.ds(sb * ls + t, 1), :]
                os_ref[pl.ds(sb * ls + t, 1), :] = (o * _silu(gate)).astype(os_ref.dtype)

    for p in range(pairs):
        qp = q_ref[:, p * LANES:(p + 1) * LANES] * (-(SB_HEAD_DIM ** -0.5))
        qq_ref[p] = jnp.concatenate([jnp.where(even_q, qp, 0.0), jnp.where(even_q, 0.0, qp)],
                                    axis=0).astype(BF16)
    carry_ref[...] = jnp.zeros_like(carry_ref)
    acc_ref[...] = jnp.zeros_like(acc_ref)

    def key_block(j, diagonal):
        start = pl.multiple_of(j * tq, tq)

        def live_rows(h):
            if not diagonal or h == 0:
                return [slice(0, 2 * tq)]
            return [slice(e * tq + h * sub, (e + 1) * tq) for e in range(2)]

        def mask_of(h, rs):
            if not diagonal:
                return None
            shape = (rs.stop - rs.start, sub)
            stacked_row = rs.start + lax.broadcasted_iota(jnp.int32, shape, 0)
            query = jnp.where(stacked_row < tq, stacked_row, stacked_row - tq)
            return lax.broadcasted_iota(jnp.int32, shape, 1) + h * sub < query

        for p in range(pairs):
            kt = kt_ref[0, 0, p * LANES:(p + 1) * LANES, pl.ds(start, tq)].astype(BF16)
            nbias = jnp.where(top, -bias_ref[2 * p], -bias_ref[2 * p + 1])
            nz = jnp.dot(qq_ref[p], kt, preferred_element_type=F32)
            for h in range(halves):
                nz_ref[p, :, h * sub:(h + 1) * sub] = nz[:, h * sub:(h + 1) * sub] + nbias
        for p in range(pairs):
            for h in range(halves):
                for rs in live_rows(h):
                    hl_ref[p, h, rs, :] = _sb_log_keep(nz_ref[p, rs, h * sub:(h + 1) * sub],
                                                       mask_of(h, rs))
        for p in range(pairs):
            for h in range(halves):
                for rs in live_rows(h):
                    r_ref[p, h, rs, :] = jnp.dot(hl_ref[p, h, rs, :], tri_ref[...],
                                                 preferred_element_type=F32)
        for p in range(pairs):
            carry = carry_ref[p]
            for h in reversed(range(halves)):
                pieces, done = [], 0
                for rs in live_rows(h):
                    a, moved = _sb_weights(r_ref[p, h, rs, :], carry[rs],
                                           nz_ref[p, rs, h * sub:(h + 1) * sub], mask_of(h, rs))
                    ab = a.astype(BF16)
                    if rs.start > done:
                        pieces.append(carry[done:rs.start])
                    pieces.append(moved)
                    done = rs.stop
                    for e in range(2):
                        lo, hi = max(rs.start, e * tq), min(rs.stop, (e + 1) * tq)
                        if lo < hi:
                            lanes = slice((2 * h + e) * sub, (2 * h + e + 1) * sub)
                            dead = lo - e * tq
                            if dead:
                                aa_ref[p, 0:dead, lanes] = jnp.zeros((dead, sub), BF16)
                            aa_ref[p, dead:tq, lanes] = ab[lo - rs.start:hi - rs.start]
                carry = jnp.concatenate(pieces, axis=0)
            carry_ref[p] = carry
        for p in range(pairs):
            vb = v_ref[pl.ds(start, tq), p * LANES:(p + 1) * LANES]
            vv = []
            for h in range(halves):
                vh = vb[h * sub:(h + 1) * sub]
                vv += [jnp.where(even_v, vh, zero), jnp.where(even_v, zero, vh)]
            acc_ref[p] += jnp.dot(aa_ref[p], jnp.concatenate(vv, axis=0),
                                  preferred_element_type=F32)

    first = b * (tiles * (tiles + 1) // 2) + i * (i + 1) // 2
    sample_fetch(first)
    key_block(i, True)
    sample_pages(first)
    sample_finish(first)

    def step(it, _):
        at = first + 1 + it
        sample_fetch(at)
        key_block(i - 1 - it, False)
        sample_pages(at)
        sample_finish(at)
        return 0

    lax.fori_loop(0, i, step, 0)
    for p in range(pairs):
        sl = slice(p * LANES, (p + 1) * LANES)
        o_ref[:, sl] = (acc_ref[p] * _silu(g_ref[:, sl])).astype(o_ref.dtype)

    @pl.when((b == pl.num_programs(0) - 1) & (i == tiles - 1))
    def _():
        def rest(at, _):
            sample_fetch(at)
            sample_pages(at)
            sample_finish(at)
            return 0

        lax.fori_loop(pl.num_programs(0) * (tiles * (tiles + 1) // 2), total, rest, 0)


def sb_attend(q, kt, v, g, qs, ks_new, vs_new, gs, bias, cache_kt, cache_vt, page_table, layer, ls):
    _, n, w, l = kt.shape
    m_s = qs.shape[0]
    n_seq, n_pages = page_table.shape
    assert n_seq * (n_pages // SB_SAMPLE_PAGES) >= SB_SAMPLE_RING
    tq = SB_PROMPT_BLOCK
    sub = LANES
    halves = tq // sub
    pairs = SB_HEADS // 2
    tiles = l // tq
    pages = SB_SAMPLE_PAGES
    rows = ls * SB_HEADS
    blk = PAGE_SIZE
    tile = pl.BlockSpec((tq, w), lambda b, i, pt: (b * tiles + i, 0))
    whole = pl.BlockSpec((m_s, w), lambda b, i, pt: (0, 0))
    nbias_rows = jnp.broadcast_to(-jnp.tile(bias, ls)[:, None], (rows, blk)).astype(F32)
    grid_spec = pltpu.PrefetchScalarGridSpec(
        num_scalar_prefetch=1,
        grid=(n, tiles),
        in_specs=[pl.BlockSpec(memory_space=pltpu.SMEM), tile,
                  pl.BlockSpec((1, 1, w, l), lambda b, i, pt: (layer, b, 0, 0)),
                  pl.BlockSpec((l, w), lambda b, i, pt: (b, 0)), tile,
                  pl.BlockSpec((2 * sub, 2 * sub), lambda b, i, pt: (0, 0)),
                  whole, whole, whole, whole,
                  pl.BlockSpec((rows, blk), lambda b, i, pt: (0, 0)),
                  pl.BlockSpec(memory_space=pl.ANY), pl.BlockSpec(memory_space=pl.ANY)],
        out_specs=[tile, whole],
        scratch_shapes=[pltpu.VMEM((pairs, 2 * tq, LANES), BF16),
                        pltpu.VMEM((pairs, 2 * tq, tq), F32),
                        pltpu.VMEM((pairs, halves, 2 * tq, 2 * sub), BF16),
                        pltpu.VMEM((pairs, halves, 2 * tq, 2 * sub), F32),
                        pltpu.VMEM((pairs, tq, halves * 2 * sub), BF16),
                        pltpu.VMEM((pairs, 2 * tq, sub), F32),
                        pltpu.VMEM((pairs, tq, LANES), F32),
                        pltpu.VMEM((SB_SAMPLE_RING, pages, w, blk), F32),
                        pltpu.VMEM((SB_SAMPLE_RING, pages, w, blk), F32),
                        pltpu.SemaphoreType.DMA((SB_SAMPLE_RING, 2 * pages)),
                        pltpu.VMEM((rows, w), BF16), pltpu.VMEM((rows, blk), F32),
                        pltpu.VMEM((rows, w), F32), pltpu.VMEM((blk, w), F32),
                        pltpu.VMEM((w, pages * blk), BF16), pltpu.VMEM((w, pages * blk), BF16),
                        pltpu.VMEM((rows, pages * blk), F32),
                        pltpu.VMEM((pages * rows, 2 * blk), BF16),
                        pltpu.VMEM((pages * rows, 2 * blk), F32),
                        pltpu.VMEM((rows, pages * blk), BF16)],
    )
    return pl.pallas_call(
        functools.partial(_sb_body, ls=ls, pages=pages, n_pages=n_pages, n_seq=n_seq, layer=layer),
        grid_spec=grid_spec,
        out_shape=[jax.ShapeDtypeStruct((n * l, w), BF16), jax.ShapeDtypeStruct((m_s, w), F32)],
        compiler_params=_params("arbitrary", "arbitrary"),
        name="sb_attend",
    )(page_table.reshape(-1), bias, q, kt, v, g, _suffix_sum_matrix(sub),
      qs, ks_new, vs_new, gs, nbias_rows, cache_kt, cache_vt)


def _softmax_rows(s):
    e = jnp.exp(s - jnp.max(s, axis=-1, keepdims=True))
    return e / jnp.sum(e, axis=-1, keepdims=True)


def _mem_body(q_ref, g_ref, mk_ref, mv_ref, o_ref, *, t, nb):
    nt = (((1,), (1,)), ((), ()))
    for bi in range(nb):
        seq = slice(bi * t, (bi + 1) * t)
        for h in range(MEM_HEADS):
            sl = slice(h * MEM_HEAD_DIM, (h + 1) * MEM_HEAD_DIM)
            s = lax.dot_general(q_ref[seq, sl].astype(BF16), mk_ref[bi, :, sl].astype(BF16), nt,
                                preferred_element_type=F32) * (MEM_HEAD_DIM ** -0.5)
            o = jnp.dot(_softmax_rows(s).astype(BF16), mv_ref[bi, :, sl].astype(BF16),
                        preferred_element_type=F32)
            o_ref[seq, sl] = (o * _silu(g_ref[seq, sl])).astype(o_ref.dtype)


def _mem_paired_body(q_ref, g_ref, mk_ref, mv_ref, o_ref, *, t, nb):
    nt = (((1,), (1,)), ((), ()))
    n_rows = mk_ref.shape[2] * mk_ref.shape[3]
    row_head = lax.broadcasted_iota(jnp.int32, (MEM_HEADS * t, n_rows), 1) % MEM_HEADS
    q_head = lax.broadcasted_iota(jnp.int32, (MEM_HEADS * t, n_rows), 0) // t
    for bi in range(nb):
        seq = slice(bi * t, (bi + 1) * t)
        heads = [slice(h * MEM_HEAD_DIM, (h + 1) * MEM_HEAD_DIM) for h in range(MEM_HEADS)]
        q = jnp.concatenate([q_ref[seq, sl] for sl in heads], axis=0).astype(BF16)
        mk = mk_ref[0, bi].reshape(n_rows, MEM_HEAD_DIM).astype(BF16)
        mv = mv_ref[0, bi].reshape(n_rows, MEM_HEAD_DIM).astype(BF16)
        s = lax.dot_general(q, mk, nt, preferred_element_type=F32) * (MEM_HEAD_DIM ** -0.5)
        p = _softmax_rows(jnp.where(row_head == q_head, s, -jnp.inf))
        o = jnp.dot(p.astype(BF16), mv, preferred_element_type=F32)
        for h, sl in enumerate(heads):
            o_ref[seq, sl] = (o[h * t:(h + 1) * t] * _silu(g_ref[seq, sl])).astype(o_ref.dtype)


def mem_attend(q, g, mk, mv, n, l, t, nb, out_dtype, layer=None):
    w = q.shape[1]
    tiles = l // t
    tile = pl.BlockSpec((nb * t, w), lambda b, i: (b * tiles + i, 0))
    if layer is None:
        mem = pl.BlockSpec((nb,) + mk.shape[1:], lambda b, i: (b, 0, 0))
    else:
        mem = pl.BlockSpec((1, nb) + mk.shape[2:], lambda b, i: (layer, b, 0, 0, 0))
    return pl.pallas_call(
        functools.partial(_mem_body if layer is None else _mem_paired_body, t=t, nb=nb),
        grid=(n // nb, tiles),
        in_specs=[tile, tile, mem, mem],
        out_specs=tile,
        out_shape=jax.ShapeDtypeStruct((n * l, w), out_dtype),
        compiler_params=_params("parallel", "arbitrary"),
        name="mem_attend",
    )(q, g, mk, mv)


def _out_proj_body(c_ref, p_ref, s_ref, m_ref, w_ref, pg_ref, x_ref, fg_ref, o_ref, *, final):
    acc = None
    for gi, part in enumerate((c_ref, p_ref, s_ref, m_ref)):
        d = jnp.dot(part[...].astype(BF16), w_ref[gi * GROUP_W:(gi + 1) * GROUP_W, :],
                    preferred_element_type=F32)
        acc = d if acc is None else acc + d
    y = x_ref[...] + _rms(acc, pg_ref[...])
    if final:
        y = _rms(y, fg_ref[...])
    o_ref[...] = y


def out_proj(parts, w, post_g, x, final_g, tm, final):
    m, d = x.shape
    part = pl.BlockSpec((tm, GROUP_W), lambda i: (i, 0))
    vec = pl.BlockSpec((1, d), lambda i: (0, 0))
    row = pl.BlockSpec((tm, d), lambda i: (i, 0))
    return pl.pallas_call(
        functools.partial(_out_proj_body, final=final),
        grid=(m // tm,),
        in_specs=[part, part, part, part, pl.BlockSpec(w.shape, lambda i: (0, 0)), vec, row, vec],
        out_specs=row,
        out_shape=jax.ShapeDtypeStruct((m, d), F32),
        compiler_params=_params("parallel"),
        name="out_proj",
    )(*parts, w, post_g.reshape(1, d), x, final_g.reshape(1, d))


def _project(x, l, lw, tm, keys_on_lanes, layer=0, depth=1, stacked=()):
    plan = [(c, "rows", F32) for c in range(len(MIX_GROUPS))]
    if keys_on_lanes:
        for c, name in enumerate(MIX_GROUPS):
            if name in BF16_GROUPS:
                plan[c] = (c, "rows", BF16)
        plan[K_GROUP] = (K_GROUP, "cols", F32)
        plan[V_GROUP:V_GROUP + 1] = [(V_GROUP, "cols", F32), (V_GROUP, "rows", BF16)]
    z = list(rms_proj(x, lw[0], lw[2], tm, tuple(plan), l, layer, depth, stacked))
    v_rows = z.pop(V_GROUP + 1) if keys_on_lanes else z[V_GROUP]
    return z, v_rows


def _mix(x, z, y_s, n, l, conv_hist, pool_hist, pos0, mem_fn, lw, final_g, final, tiles):
    (_, post_g, _, w_dw, b_dw, ln_g, ln_b, w_pw, b_pw, pool_w, pool_scale, _, w_out) = lw
    _, tm_out, t_local, nb, mix_dtype = tiles
    ca, cb, cg, pu, pg, _, _, _, _, mq, mg = z
    y_c, conv_new = conv_mixer(ca, cb, cg, conv_hist, w_dw, b_dw, ln_g, ln_b, w_pw, b_pw, n, l,
                               t_local, nb, mix_dtype)
    y_p, pool_new = pool_mixer(pu, pg, pool_hist, pool_w, pool_scale, n, l, t_local, nb, pos0,
                               mix_dtype)
    y_m = mem_fn(mq, mg, t_local, nb, mix_dtype)
    x = out_proj([y_c, y_p, y_s, y_m], w_out, post_g, x, final_g, tm_out, final)
    return x, conv_new, pool_new


def kernel(x_prompt, x_sample, mem_prompt, cache_k, cache_v, cache_mem_k, cache_mem_v, state_conv,
           state_pool, page_table, pre_g, post_g, w_in, conv_w_dw, conv_b_dw, conv_ln_g, conv_ln_b,
           conv_w_pw, conv_b_pw, pool_w, pool_scale, sb_bias, mem_g, w_mem_kv, w_out, final_g):
    bp, lp, d = x_prompt.shape
    bs, ls, _ = x_sample.shape
    depth = w_in.shape[0]
    n_mem = mem_prompt.shape[1]
    n_phys = cache_k.shape[1]
    past = page_table.shape[1] * PAGE_SIZE
    w = GROUP_W

    xp = x_prompt.reshape(bp * lp, d)
    xs = x_sample.reshape(bs * ls, d)
    mem = mem_prompt.reshape(bp * n_mem, d)
    ckt = jnp.transpose(cache_k, (0, 1, 3, 4, 2)).reshape(depth, n_phys, w, PAGE_SIZE)
    cvt = jnp.transpose(cache_v, (0, 1, 3, 4, 2)).reshape(depth, n_phys, w, PAGE_SIZE)
    pair_shape = (depth, bs, n_mem // 2, 2 * MEM_HEADS, MEM_HEAD_DIM)
    cmk = cache_mem_k.reshape(pair_shape)
    cmv = cache_mem_v.reshape(pair_shape)
    zero_conv = jnp.zeros((bp, CONV_HIST, w), F32)
    zero_pool = jnp.zeros((bp, POOL_HIST, w), F32)
    prompt_tiles = (512, 512, 512, 1, BF16)
    sample_tiles = (bs * ls, bs * ls, ls, 8, F32)

    outs = [[] for _ in range(8)]
    kv_p = ()
    for l in range(depth):
        final = l == depth - 1
        lw = (pre_g[l], post_g[l], w_in[l].astype(BF16), conv_w_dw[l], conv_b_dw[l], conv_ln_g[l],
              conv_ln_b[l], conv_w_pw[l].astype(BF16), conv_b_pw[l], pool_w[l].astype(BF16),
              pool_scale[l], sb_bias[l], w_out[l].astype(BF16))
        mk_p, mv_p = [a.reshape(bp, n_mem, w)
                      for a in rms_proj(mem, mem_g[l], w_mem_kv[l].astype(BF16), 256)]

        def mem_prompt_fn(q, g, t, nb, dtype, mk_p=mk_p, mv_p=mv_p):
            return mem_attend(q, g, mk_p, mv_p, bp, lp, t, nb, dtype)

        def mem_sample_fn(q, g, t, nb, dtype, l=l):
            return mem_attend(q, g, cmk, cmv, bs, ls, t, nb, dtype, layer=l)

        zp, vp_rows = _project(xp, lp, lw, prompt_tiles[0], True, l, depth, kv_p)
        zs, _ = _project(xs, ls, lw, sample_tiles[0], False)
        q_at, k_at, v_at, g_at = (MIX_GROUPS.index(name) for name in ("sb_q", "sb_k", "sb_v", "sb_gate"))
        k_p, v_p, k_s, v_s = zp[k_at], zp[v_at], zs[k_at], zs[v_at]
        kv_p = (k_p, v_p)
        ys_p, ys_s = sb_attend(zp[q_at], k_p, vp_rows, zp[g_at], zs[q_at], k_s, v_s, zs[g_at],
                               sb_bias[l], ckt, cvt, page_table, l, ls)
        xp, c_p, p_p = _mix(xp, zp, ys_p, bp, lp, zero_conv, zero_pool, 0, mem_prompt_fn, lw,
                            final_g, final, prompt_tiles)
        xs, c_s, p_s = _mix(xs, zs, ys_s, bs, ls, state_conv[l], state_pool[l], past,
                            mem_sample_fn, lw, final_g, final, sample_tiles)
        for lst, a in zip(outs, (k_s, v_s, c_p, c_s, p_p, p_s, mk_p, mv_p)):
            lst.append(a)

    k_s, v_s, c_p, c_s, p_p, p_s, mk_p, mv_p = [jnp.stack(a) for a in outs]
    k_p, v_p = kv_p
    heads_first = (depth, bp, SB_HEADS, SB_HEAD_DIM, lp)
    return (xp.reshape(bp, lp, d), xs.reshape(bs, ls, d),
            jnp.transpose(k_p.reshape(heads_first), (0, 1, 4, 2, 3)),
            jnp.transpose(v_p.reshape(heads_first), (0, 1, 4, 2, 3)),
            k_s.reshape(depth, bs, ls, SB_HEADS, SB_HEAD_DIM),
            v_s.reshape(depth, bs, ls, SB_HEADS, SB_HEAD_DIM),
            c_p, c_s, p_p, p_s,
            mk_p.reshape(depth, bp, n_mem, MEM_HEADS, MEM_HEAD_DIM),
            mv_p.reshape(depth, bp, n_mem, MEM_HEADS, MEM_HEAD_DIM))
```
